```python
import math
import jax, jax.numpy as jnp
from jax import lax
import numpy as np

D_MODEL = 2048
BATCH = 1
SEQ = 8192
DEPTH = 1

CHUNK = 64
Q_BLOCK = 128
EPS = 1e-6
SB_HEADS = 8
SB_HEAD_DIM = 128
SB_WIDTH = SB_HEADS * SB_HEAD_DIM
MLA_HEADS = 8
MLA_NOPE_DIM = 128
MLA_ROPE_DIM = 64
MLA_QK_DIM = MLA_NOPE_DIM + MLA_ROPE_DIM
MLA_V_DIM = 128
MLA_WIDTH = MLA_HEADS * MLA_V_DIM
Q_LORA_RANK = 512
KV_LORA_RANK = 256
ROPE_THETA = 10000.0
D_MIX = SB_WIDTH + MLA_WIDTH
IN_SPLITS = (SB_WIDTH, SB_WIDTH, SB_WIDTH, SB_WIDTH,
             Q_LORA_RANK, KV_LORA_RANK, MLA_ROPE_DIM, MLA_WIDTH)
D_IN = sum(IN_SPLITS)

kernel_name = 'hybrid_stickbreak_mla_block'


def rms_norm(x, w):
    xf = x.astype(jnp.float32)
    y = xf * lax.rsqrt(jnp.mean(xf * xf, axis=-1, keepdims=True) + EPS)
    return (y * w.astype(jnp.float32)).astype(x.dtype)


def rope_tables(positions):
    inv_freq = ROPE_THETA ** (-jnp.arange(0, MLA_ROPE_DIM, 2, dtype=jnp.float32) / MLA_ROPE_DIM)
    ang = positions.astype(jnp.float32)[..., None] * inv_freq
    return jnp.cos(ang), jnp.sin(ang)


def apply_rope(x, cos, sin):
    x1, x2 = jnp.split(x.astype(jnp.float32), 2, axis=-1)
    out = jnp.concatenate([x1 * cos - x2 * sin, x2 * cos + x1 * sin], axis=-1)
    return out.astype(x.dtype)


def to_heads(t, n_heads):
    b, s, _ = t.shape
    return t.reshape(b, s, n_heads, -1).transpose(0, 2, 1, 3)


def from_heads(t):
    b, h, s, d = t.shape
    return t.transpose(0, 2, 1, 3).reshape(b, s, h * d)


def stick_breaking_attention(q, k, v):
    seq, d = q.shape[2], q.shape[3]
    scale = 1.0 / math.sqrt(d)
    outs = []
    for b0 in range(0, seq, Q_BLOCK):
        kl = b0 + Q_BLOCK
        z = jnp.einsum('bhqd,bhkd->bhqk', q[:, :, b0:kl], k[:, :, :kl]).astype(jnp.float32) * scale
        t_idx = b0 + jnp.arange(Q_BLOCK)[:, None]
        s_idx = jnp.arange(kl)[None, :]
        before = s_idx < t_idx
        log_keep = jnp.where(before, jax.nn.log_sigmoid(-z), 0.0)
        later = lax.cumsum(log_keep, axis=3, reverse=True) - log_keep
        a = jnp.where(before, jnp.exp(jax.nn.log_sigmoid(z) + later), 0.0)
        outs.append(jnp.einsum('bhqk,bhkd->bhqd', a.astype(v.dtype), v[:, :, :kl]))
    return jnp.concatenate(outs, axis=2)


def chunk_causal_softmax_attention(q, k, v):
    seq, d = q.shape[2], q.shape[3]
    scale = 1.0 / math.sqrt(d)
    outs = []
    for b0 in range(0, seq, Q_BLOCK):
        kl = b0 + Q_BLOCK
        s_ = jnp.einsum('bhqd,bhkd->bhqk', q[:, :, b0:kl], k[:, :, :kl]).astype(jnp.float32) * scale
        t_chunk = (b0 + jnp.arange(Q_BLOCK))[:, None] // CHUNK
        s_chunk = jnp.arange(kl)[None, :] // CHUNK
        s_ = jnp.where(s_chunk <= t_chunk, s_, -jnp.inf)
        p = jax.nn.softmax(s_, axis=-1)
        outs.append(jnp.einsum('bhqk,bhkd->bhqd', p.astype(v.dtype), v[:, :, :kl]))
    return jnp.concatenate(outs, axis=2)


def hybrid_layer(x, cos, sin, pre_norm_w, w_in, q_norm_w, w_q_up, kv_norm_w, w_kv_up, w_out, post_norm_w):
    b, s, _ = x.shape
    h = rms_norm(x, pre_norm_w)
    proj = h @ w_in
    split_pts = tuple(int(i) for i in np.cumsum(IN_SPLITS)[:-1])
    sb_q, sb_k, sb_v, sb_gate, c_q, c_kv, k_rope, mla_gate = jnp.split(proj, split_pts, axis=-1)

    o_a = from_heads(stick_breaking_attention(to_heads(sb_q, SB_HEADS),
                                              to_heads(sb_k, SB_HEADS),
                                              to_heads(sb_v, SB_HEADS)))

    q_full = (rms_norm(c_q, q_norm_w) @ w_q_up).reshape(b, s, MLA_HEADS, MLA_QK_DIM)
    q_nope, q_rot = jnp.split(q_full, [MLA_NOPE_DIM], axis=-1)
    q_rot = apply_rope(q_rot, cos[:, :, None, :], sin[:, :, None, :])
    kv = (rms_norm(c_kv, kv_norm_w) @ w_kv_up).reshape(b, s, MLA_HEADS, MLA_NOPE_DIM + MLA_V_DIM)
    k_nope, v_mla = jnp.split(kv, [MLA_NOPE_DIM], axis=-1)
    k_rot = apply_rope(k_rope, cos, sin)
    k_rot = jnp.broadcast_to(k_rot[:, :, None, :], (b, s, MLA_HEADS, MLA_ROPE_DIM))
    q_mla = jnp.concatenate([q_nope, q_rot], axis=-1).transpose(0, 2, 1, 3)
    k_mla = jnp.concatenate([k_nope, k_rot], axis=-1).transpose(0, 2, 1, 3)
    o_b = from_heads(chunk_causal_softmax_attention(q_mla, k_mla, v_mla.transpose(0, 2, 1, 3)))

    mixed = jnp.concatenate([o_a * jax.nn.silu(sb_gate), o_b * jax.nn.silu(mla_gate)], axis=-1)
    y = mixed @ w_out
    return x + rms_norm(y, post_norm_w)


def setup_inputs(seed: int = 0) -> dict:
    key = jax.random.key(seed)
    ks = jax.random.split(key, 12)
    f32 = jnp.float32
    x = jax.random.normal(ks[0], (BATCH, SEQ, D_MODEL), f32)
    positions = jnp.broadcast_to(jnp.arange(SEQ, dtype=jnp.int32)[None, :], (BATCH, SEQ))
    pre_norm_w = 1.0 + 0.05 * jax.random.normal(ks[1], (DEPTH, D_MODEL), f32)
    w_in = jax.random.normal(ks[2], (DEPTH, D_MODEL, D_IN), f32) * D_MODEL ** -0.5
    q_norm_w = 1.0 + 0.05 * jax.random.normal(ks[3], (DEPTH, Q_LORA_RANK), f32)
    w_q_up = jax.random.normal(ks[4], (DEPTH, Q_LORA_RANK, MLA_HEADS * MLA_QK_DIM), f32) * Q_LORA_RANK ** -0.5
    kv_norm_w = 1.0 + 0.05 * jax.random.normal(ks[5], (DEPTH, KV_LORA_RANK), f32)
    w_kv_up = jax.random.normal(ks[6], (DEPTH, KV_LORA_RANK, MLA_HEADS * (MLA_NOPE_DIM + MLA_V_DIM)), f32) * KV_LORA_RANK ** -0.5
    w_out = jax.random.normal(ks[7], (DEPTH, D_MIX, D_MODEL), f32) * D_MIX ** -0.5
    post_norm_w = 1.0 + 0.05 * jax.random.normal(ks[8], (DEPTH, D_MODEL), f32)
    return {'x': x, 'positions': positions, 'pre_norm_w': pre_norm_w, 'w_in': w_in,
            'q_norm_w': q_norm_w, 'w_q_up': w_q_up, 'kv_norm_w': kv_norm_w, 'w_kv_up': w_kv_up,
            'w_out': w_out, 'post_norm_w': post_norm_w}


def reference(x, positions, pre_norm_w, w_in, q_norm_w, w_q_up, kv_norm_w, w_kv_up, w_out, post_norm_w):
    cos, sin = rope_tables(positions)
    for i in range(DEPTH):
        x = hybrid_layer(x, cos, sin, pre_norm_w[i], w_in[i], q_norm_w[i], w_q_up[i],
                         kv_norm_w[i], w_kv_up[i], w_out[i], post_norm_w[i])
    return x
```

```python
import functools
import math

import numpy as np
import jax
import jax.numpy as jnp
from jax import lax
from jax.experimental import pallas as pl
from jax.experimental.pallas import tpu as pltpu

F32 = jnp.float32
BF16 = jnp.bfloat16

LANES = 128
VMEM_LIMIT_BYTES = 56 * 1024 * 1024

CHUNK = 64
EPS = 1e-6
SB_HEADS = 8
SB_HEAD_DIM = 128
SB_WIDTH = SB_HEADS * SB_HEAD_DIM
MLA_HEADS = 8
MLA_NOPE_DIM = 128
MLA_ROPE_DIM = 64
MLA_QK_DIM = MLA_NOPE_DIM + MLA_ROPE_DIM
MLA_V_DIM = 128
MLA_WIDTH = MLA_HEADS * MLA_V_DIM
Q_LORA_RANK = 512
KV_LORA_RANK = 256
ROPE_THETA = 10000.0
HALF_ROPE = MLA_ROPE_DIM // 2
MLA_QK_PAD = 2 * LANES

MAIN_WIDTH = 4 * SB_WIDTH + MLA_WIDTH
TAIL_WIDTH = Q_LORA_RANK + KV_LORA_RANK + 2 * LANES
SB_Q_BLK, SB_K_BLK, SB_V_BLK, SB_G_BLK, MLA_G_BLK = (g * SB_HEADS for g in range(5))


def _rms_norm(x, w):
    return x * lax.rsqrt(jnp.mean(x * x, axis=-1, keepdims=True) + EPS) * w


def _silu(g):
    return g * (1.0 / (1.0 + jnp.exp(-g)))


def _dot_nt(a, b):
    return lax.dot_general(a, b, (((1,), (1,)), ((), ())), preferred_element_type=F32)


def _in_proj_kernel(x_ref, nw_ref, w_ref, main_ref, tail_ref, h_ref, *, n_main, n_scaled, q_scale):
    j = pl.program_id(1)

    @pl.when(j == 0)
    def _():
        h_ref[...] = _rms_norm(x_ref[...], nw_ref[...]).astype(BF16)

    def proj():
        return jnp.dot(h_ref[...], w_ref[...], preferred_element_type=F32)

    @pl.when(j < n_scaled)
    def _():
        main_ref[...] = (proj() * q_scale).astype(BF16)

    @pl.when((j >= n_scaled) & (j < n_main))
    def _():
        main_ref[...] = proj().astype(BF16)

    @pl.when(j >= n_main)
    def _():
        tail_ref[...] = proj()


def _in_proj(x, norm_w, w_cat, *, tm, tn):
    s, d = x.shape
    n_main = MAIN_WIDTH // tn
    n_tail = TAIL_WIDTH // tn
    kern = functools.partial(_in_proj_kernel, n_main=n_main, n_scaled=SB_WIDTH // tn,
                             q_scale=1.0 / math.sqrt(SB_HEAD_DIM))
    return pl.pallas_call(
        kern,
        grid=(s // tm, n_main + n_tail),
        in_specs=[
            pl.BlockSpec((tm, d), lambda i, j: (i, 0)),
            pl.BlockSpec((1, d), lambda i, j: (0, 0)),
            pl.BlockSpec((d, tn), lambda i, j: (0, j)),
        ],
        out_specs=[
            pl.BlockSpec((tm, tn), lambda i, j: (i, jnp.minimum(j, n_main - 1))),
            pl.BlockSpec((tm, tn), lambda i, j: (i, jnp.maximum(j - n_main, 0))),
        ],
        out_shape=[jax.ShapeDtypeStruct((s, MAIN_WIDTH), BF16),
                   jax.ShapeDtypeStruct((s, TAIL_WIDTH), F32)],
        scratch_shapes=[pltpu.VMEM((tm, d), BF16)],
        compiler_params=pltpu.CompilerParams(
            dimension_semantics=("arbitrary", "arbitrary"), vmem_limit_bytes=VMEM_LIMIT_BYTES),
        name="in_proj",
    )(x, norm_w, w_cat)


def _rope_table_kernel(pos_ref, freq_ref, cos_ref, sin_ref):
    ang = pos_ref[...].astype(F32) * freq_ref[...]
    cos_ref[...] = jnp.cos(ang)
    sin_ref[...] = jnp.sin(ang)


def _rope_tables(positions):
    s = positions.shape[0]
    per_row = LANES // HALF_ROPE
    inv_freq = ROPE_THETA ** (-jnp.arange(0, MLA_ROPE_DIM, 2, dtype=F32) / MLA_ROPE_DIM)
    pos_rep = jnp.repeat(positions.reshape(s // per_row, per_row), HALF_ROPE, axis=1)
    freq_rep = jnp.tile(inv_freq, per_row).reshape(1, LANES)
    shape = jax.ShapeDtypeStruct((s // per_row, LANES), F32)
    cos, sin = pl.pallas_call(_rope_table_kernel, out_shape=[shape, shape], name="rope_tab")(
        pos_rep, freq_rep)
    zeros = jnp.zeros((s, LANES - MLA_ROPE_DIM), F32)

    def widen(t):
        t = t.reshape(s, HALF_ROPE)
        return jnp.concatenate([t, t, zeros], axis=1)

    return widen(cos), widen(sin)


def _mla_prep_kernel(tail_ref, cos_ref, sin_ref, qnw_ref, kvnw_ref, wq_ref, wqr_ref, wkv_ref,
                     q_ref, k_ref, v_ref, *, q_scale):
    cos = cos_ref[...]
    sin = sin_ref[...]
    c_q = tail_ref[:, 0:Q_LORA_RANK]
    hq = _rms_norm(c_q, qnw_ref[...]).astype(BF16)
    zq = jnp.dot(hq, wq_ref[...], preferred_element_type=F32)
    zr = jnp.dot(hq, wqr_ref[...], preferred_element_type=F32)
    for h in range(MLA_HEADS):
        lo = h * MLA_QK_PAD
        q_ref[:, lo:lo + LANES] = (zq[:, lo:lo + LANES] * q_scale).astype(BF16)
        rot = zq[:, lo + LANES:lo + 2 * LANES] * cos + zr[:, h * LANES:(h + 1) * LANES] * sin
        q_ref[:, lo + LANES:lo + 2 * LANES] = (rot * q_scale).astype(BF16)

    c_kv = tail_ref[:, Q_LORA_RANK:Q_LORA_RANK + KV_LORA_RANK]
    hkv = _rms_norm(c_kv, kvnw_ref[...]).astype(BF16)
    zkv = jnp.dot(hkv, wkv_ref[...], preferred_element_type=F32)
    kr0 = Q_LORA_RANK + KV_LORA_RANK
    k_rot = (tail_ref[:, kr0:kr0 + LANES] * cos + tail_ref[:, kr0 + LANES:kr0 + 2 * LANES] * sin)
    k_rot = k_rot.astype(BF16)
    for h in range(MLA_HEADS):
        lo = h * MLA_QK_PAD
        k_ref[:, lo:lo + LANES] = zkv[:, h * LANES:(h + 1) * LANES].astype(BF16)
        k_ref[:, lo + LANES:lo + 2 * LANES] = k_rot
    v_ref[...] = zkv[:, MLA_HEADS * MLA_NOPE_DIM:].astype(BF16)


def _mla_prep(tail, cos, sin, q_norm_w, kv_norm_w, wq, wqr, wkv, *, tm):
    s = tail.shape[0]
    row = lambda i: (i, 0)
    fixed = lambda i: (0, 0)
    kern = functools.partial(_mla_prep_kernel, q_scale=1.0 / math.sqrt(MLA_QK_DIM))
    return pl.pallas_call(
        kern,
        grid=(s // tm,),
        in_specs=[
            pl.BlockSpec((tm, TAIL_WIDTH), row),
            pl.BlockSpec((tm, LANES), row),
            pl.BlockSpec((tm, LANES), row),
            pl.BlockSpec(q_norm_w.shape, fixed),
            pl.BlockSpec(kv_norm_w.shape, fixed),
            pl.BlockSpec(wq.shape, fixed),
            pl.BlockSpec(wqr.shape, fixed),
            pl.BlockSpec(wkv.shape, fixed),
        ],
        out_specs=[
            pl.BlockSpec((tm, MLA_HEADS * MLA_QK_PAD), row),
            pl.BlockSpec((tm, MLA_HEADS * MLA_QK_PAD), row),
            pl.BlockSpec((tm, MLA_WIDTH), row),
        ],
        out_shape=[jax.ShapeDtypeStruct((s, MLA_HEADS * MLA_QK_PAD), BF16),
                   jax.ShapeDtypeStruct((s, MLA_HEADS * MLA_QK_PAD), BF16),
                   jax.ShapeDtypeStruct((s, MLA_WIDTH), BF16)],
        compiler_params=pltpu.CompilerParams(
            dimension_semantics=("arbitrary",), vmem_limit_bytes=VMEM_LIMIT_BYTES),
        name="mla_prep",
    )(tail, cos, sin, q_norm_w, kv_norm_w, wq, wqr, wkv)


def _sb_attn_kernel(q_ref, k_ref, v_ref, g_ref, o_ref, tri_ref, acc_ref, carry_ref, *, tq, tk):
    i = pl.program_id(1)
    q = q_ref[...]

    rows = lax.broadcasted_iota(jnp.int32, (tk, tk), 0)
    cols = lax.broadcasted_iota(jnp.int32, (tk, tk), 1)
    tri_ref[...] = jnp.where(rows > cols, 1.0, 0.0).astype(BF16)

    def block(start, carry, masked):
        k = k_ref[pl.ds(start, tk), :]
        v = v_ref[pl.ds(start, tk), :]
        z = _dot_nt(q, k)
        log_keep = jnp.minimum(-z, 0.0) - jnp.log(1.0 + jnp.exp(-jnp.abs(z)))
        if masked:
            t_idx = i * tq + lax.broadcasted_iota(jnp.int32, (tq, tk), 0)
            s_idx = start + lax.broadcasted_iota(jnp.int32, (tq, tk), 1)
            before = s_idx < t_idx
            log_keep = jnp.where(before, log_keep, 0.0)
        hi = log_keep.astype(BF16)
        lo = (log_keep - hi.astype(F32)).astype(BF16)
        tri = tri_ref[...]
        later = (jnp.dot(hi, tri, preferred_element_type=F32)
                 + jnp.dot(lo, tri, preferred_element_type=F32))
        a = jnp.exp(z + log_keep + later + carry)
        if masked:
            a = jnp.where(before, a, 0.0)
        pv = jnp.dot(a.astype(BF16), v, preferred_element_type=F32)
        return pv, carry + later[:, 0:1] + log_keep[:, 0:1]

    n_diag = tq // tk
    carry = jnp.zeros((tq, 1), F32)
    acc = jnp.zeros((tq, SB_HEAD_DIM), F32)
    for d in reversed(range(n_diag)):
        pv, carry = block(pl.multiple_of(i * tq + d * tk, tk), carry, True)
        acc = acc + pv
    acc_ref[...] = acc
    carry_ref[...] = carry

    def body(n, _):
        j = i * n_diag - 1 - n
        pv, carry = block(pl.multiple_of(j * tk, tk), carry_ref[...], False)
        acc_ref[...] += pv
        carry_ref[...] = carry
        return 0

    lax.fori_loop(0, i * n_diag, body, 0)
    o_ref[...] = (acc_ref[...] * _silu(g_ref[...].astype(F32))).astype(o_ref.dtype)


def _sb_attn(main, *, tq, tk):
    s = main.shape[0]
    kern = functools.partial(_sb_attn_kernel, tq=tq, tk=tk)
    return pl.pallas_call(
        kern,
        grid=(SB_HEADS, s // tq),
        in_specs=[
            pl.BlockSpec((tq, SB_HEAD_DIM), lambda h, i: (i, SB_Q_BLK + h)),
            pl.BlockSpec((s, SB_HEAD_DIM), lambda h, i: (0, SB_K_BLK + h)),
            pl.BlockSpec((s, SB_HEAD_DIM), lambda h, i: (0, SB_V_BLK + h)),
            pl.BlockSpec((tq, SB_HEAD_DIM), lambda h, i: (i, SB_G_BLK + h)),
        ],
        out_specs=pl.BlockSpec((tq, SB_HEAD_DIM), lambda h, i: (i, h)),
        out_shape=jax.ShapeDtypeStruct((s, SB_WIDTH), BF16),
        scratch_shapes=[pltpu.VMEM((tk, tk), BF16),
                        pltpu.VMEM((tq, SB_HEAD_DIM), F32),
                        pltpu.VMEM((tq, 1), F32)],
        compiler_params=pltpu.CompilerParams(
            dimension_semantics=("arbitrary", "arbitrary"), vmem_limit_bytes=VMEM_LIMIT_BYTES),
        name="sb_attn",
    )(main, main, main, main)


def _mla_attn_kernel(q_ref, k_ref, v_ref, g_ref, o_ref, m_ref, l_ref, acc_ref, *, tq, tk):
    i = pl.program_id(1)
    q = q_ref[...]
    m_ref[...] = jnp.full((tq, 1), -jnp.inf, F32)
    l_ref[...] = jnp.zeros((tq, 1), F32)
    acc_ref[...] = jnp.zeros((tq, MLA_V_DIM), F32)

    def block(start, diag_offset):
        k = k_ref[pl.ds(start, tk), :]
        v = v_ref[pl.ds(start, tk), :]
        s = _dot_nt(q, k)
        if diag_offset is not None:
            t_chunk = lax.broadcasted_iota(jnp.int32, (tq, tk), 0) // CHUNK
            s_chunk = (diag_offset + lax.broadcasted_iota(jnp.int32, (tq, tk), 1)) // CHUNK
            s = jnp.where(s_chunk <= t_chunk, s, -jnp.inf)
        m_prev = m_ref[...]
        m_new = jnp.maximum(m_prev, jnp.max(s, axis=-1, keepdims=True))
        alpha = jnp.exp(m_prev - m_new)
        p = jnp.exp(s - m_new)
        l_ref[...] = alpha * l_ref[...] + jnp.sum(p, axis=-1, keepdims=True)
        acc_ref[...] = alpha * acc_ref[...] + jnp.dot(p.astype(BF16), v, preferred_element_type=F32)
        m_ref[...] = m_new

    n_diag = tq // tk

    def body(j, _):
        block(pl.multiple_of(j * tk, tk), None)
        return 0

    lax.fori_loop(0, i * n_diag, body, 0)
    for d in range(n_diag):
        block(pl.multiple_of(i * tq + d * tk, tk), d * tk)

    out = acc_ref[...] * (1.0 / l_ref[...])
    o_ref[...] = (out * _silu(g_ref[...].astype(F32))).astype(o_ref.dtype)


def _mla_attn(q, k, v, main, *, tq, tk):
    s = q.shape[0]
    kern = functools.partial(_mla_attn_kernel, tq=tq, tk=tk)
    return pl.pallas_call(
        kern,
        grid=(MLA_HEADS, s // tq),
        in_specs=[
            pl.BlockSpec((tq, MLA_QK_PAD), lambda h, i: (i, h)),
            pl.BlockSpec((s, MLA_QK_PAD), lambda h, i: (0, h)),
            pl.BlockSpec((s, MLA_V_DIM), lambda h, i: (0, h)),
            pl.BlockSpec((tq, MLA_V_DIM), lambda h, i: (i, MLA_G_BLK + h)),
        ],
        out_specs=pl.BlockSpec((tq, MLA_V_DIM), lambda h, i: (i, h)),
        out_shape=jax.ShapeDtypeStruct((s, MLA_WIDTH), BF16),
        scratch_shapes=[pltpu.VMEM((tq, 1), F32),
                        pltpu.VMEM((tq, 1), F32),
                        pltpu.VMEM((tq, MLA_V_DIM), F32)],
        compiler_params=pltpu.CompilerParams(
            dimension_semantics=("arbitrary", "arbitrary"), vmem_limit_bytes=VMEM_LIMIT_BYTES),
        name="mla_attn",
    )(q, k, v, main)


def _out_proj_kernel(a_ref, b_ref, w_ref, x_ref, nw_ref, o_ref):
    y = (jnp.dot(a_ref[...], w_ref[0:SB_WIDTH, :], preferred_element_type=F32)
         + jnp.dot(b_ref[...], w_ref[SB_WIDTH:, :], preferred_element_type=F32))
    o_ref[...] = x_ref[...] + _rms_norm(y, nw_ref[...])


def _out_proj(mix_a, mix_b, w_out, x, norm_w, *, tm):
    s, d = x.shape
    row = lambda i: (i, 0)
    fixed = lambda i: (0, 0)
    return pl.pallas_call(
        _out_proj_kernel,
        grid=(s // tm,),
        in_specs=[
            pl.BlockSpec((tm, SB_WIDTH), row),
            pl.BlockSpec((tm, MLA_WIDTH), row),
            pl.BlockSpec(w_out.shape, fixed),
            pl.BlockSpec((tm, d), row),
            pl.BlockSpec((1, d), fixed),
        ],
        out_specs=pl.BlockSpec((tm, d), row),
        out_shape=jax.ShapeDtypeStruct((s, d), F32),
        compiler_params=pltpu.CompilerParams(
            dimension_semantics=("arbitrary",), vmem_limit_bytes=VMEM_LIMIT_BYTES),
        name="out_proj",
    )(mix_a, mix_b, w_out, x, norm_w)


def _rotate_half_cols(w):
    w1, w2 = jnp.split(w, 2, axis=-1)
    return jnp.concatenate([-w2, w1], axis=-1)


def _prep_in_weight(w_in):
    d = w_in.shape[0]
    splits = np.cumsum([SB_WIDTH] * 4 + [Q_LORA_RANK, KV_LORA_RANK, MLA_ROPE_DIM])
    sb_q, sb_k, sb_v, sb_g, c_q, c_kv, k_rope, mla_g = jnp.split(w_in, splits, axis=1)
    pad = jnp.zeros((d, LANES - MLA_ROPE_DIM), w_in.dtype)
    cat = jnp.concatenate([sb_q, sb_k, sb_v, sb_g, mla_g, c_q, c_kv,
                           k_rope, pad, _rotate_half_cols(k_rope), pad], axis=1)
    return cat.astype(BF16)


def _prep_q_up_weight(w_q_up):
    r = w_q_up.shape[0]
    w = w_q_up.reshape(r, MLA_HEADS, MLA_QK_DIM)
    nope, rope = w[:, :, :MLA_NOPE_DIM], w[:, :, MLA_NOPE_DIM:]
    pad = jnp.zeros((r, MLA_HEADS, LANES - MLA_ROPE_DIM), w.dtype)
    wq = jnp.concatenate([nope, rope, pad], axis=2).reshape(r, MLA_HEADS * MLA_QK_PAD)
    wqr = jnp.concatenate([_rotate_half_cols(rope), pad], axis=2).reshape(r, MLA_HEADS * LANES)
    return wq.astype(BF16), wqr.astype(BF16)


def _prep_kv_up_weight(w_kv_up):
    r = w_kv_up.shape[0]
    w = w_kv_up.reshape(r, MLA_HEADS, MLA_NOPE_DIM + MLA_V_DIM)
    k_nope = w[:, :, :MLA_NOPE_DIM].reshape(r, MLA_HEADS * MLA_NOPE_DIM)
    v = w[:, :, MLA_NOPE_DIM:].reshape(r, MLA_WIDTH)
    return jnp.concatenate([k_nope, v], axis=1).astype(BF16)


def _layer(x, cos, sin, pre_norm_w, w_in, q_norm_w, w_q_up, kv_norm_w, w_kv_up, w_out, post_norm_w):
    main, tail = _in_proj(x, pre_norm_w.reshape(1, -1), _prep_in_weight(w_in), tm=1024, tn=512)
    wq, wqr = _prep_q_up_weight(w_q_up)
    q_mla, k_mla, v_mla = _mla_prep(tail, cos, sin, q_norm_w.reshape(1, -1), kv_norm_w.reshape(1, -1),
                                    wq, wqr, _prep_kv_up_weight(w_kv_up), tm=512)
    mix_a = _sb_attn(main, tq=256, tk=256)
    mix_b = _mla_attn(q_mla, k_mla, v_mla, main, tq=512, tk=512)
    return _out_proj(mix_a, mix_b, w_out.astype(BF16), x, post_norm_w.reshape(1, -1), tm=256)


def kernel(x, positions, pre_norm_w, w_in, q_norm_w, w_q_up, kv_norm_w, w_kv_up, w_out, post_norm_w):
    batch, depth = x.shape[0], pre_norm_w.shape[0]
    outs = []
    for b in range(batch):
        xb = x[b]
        cos, sin = _rope_tables(positions[b])
        for i in range(depth):
            xb = _layer(xb, cos, sin, pre_norm_w[i], w_in[i], q_norm_w[i], w_q_up[i],
                        kv_norm_w[i], w_kv_up[i], w_out[i], post_norm_w[i])
        outs.append(xb)
    return outs[0][None] if batch == 1 else jnp.stack(outs, axis=0)
```

```python
import functools
import math

import numpy as np
import jax
import jax.numpy as jnp
from jax import lax
from jax.experimental import pallas as pl
from jax.experimental.pallas import tpu as pltpu

F32 = jnp.float32
BF16 = jnp.bfloat16

LANES = 128
VMEM_LIMIT_BYTES = 56 * 1024 * 1024

CHUNK = 64
EPS = 1e-6
SB_HEADS = 8
SB_HEAD_DIM = 128
SB_WIDTH = SB_HEADS * SB_HEAD_DIM
MLA_HEADS = 8
MLA_NOPE_DIM = 128
MLA_ROPE_DIM = 64
MLA_QK_DIM = MLA_NOPE_DIM + MLA_ROPE_DIM
MLA_V_DIM = 128
MLA_WIDTH = MLA_HEADS * MLA_V_DIM
Q_LORA_RANK = 512
KV_LORA_RANK = 256
ROPE_THETA = 10000.0
HALF_ROPE = MLA_ROPE_DIM // 2
MLA_QK_PAD = 2 * LANES
LOG2_E = math.log2(math.e)
SB_SKIP_LOG2 = -150.0

MAIN_WIDTH = 4 * SB_WIDTH + MLA_WIDTH
TAIL_WIDTH = Q_LORA_RANK + KV_LORA_RANK + 2 * LANES
SB_Q_BLK, SB_K_BLK, SB_V_BLK, SB_G_BLK, MLA_G_BLK = (g * SB_HEADS for g in range(5))


def _rms_norm(x, w):
    return x * lax.rsqrt(jnp.mean(x * x, axis=-1, keepdims=True) + EPS) * w


def _silu(g):
    return g * (1.0 / (1.0 + jnp.exp(-g)))


def _dot_nt(a, b):
    return lax.dot_general(a, b, (((1,), (1,)), ((), ())), preferred_element_type=F32)


def _in_proj_kernel(x_ref, nw_ref, w_ref, main_ref, tail_ref, h_ref, *, n_main, n_scaled, q_scale):
    j = pl.program_id(1)

    @pl.when(j == 0)
    def _():
        h_ref[...] = _rms_norm(x_ref[...], nw_ref[...]).astype(BF16)

    def proj():
        return jnp.dot(h_ref[...], w_ref[...], preferred_element_type=F32)

    @pl.when(j < n_scaled)
    def _():
        main_ref[...] = (proj() * q_scale).astype(BF16)

    @pl.when((j >= n_scaled) & (j < n_main))
    def _():
        main_ref[...] = proj().astype(BF16)

    @pl.when(j >= n_main)
    def _():
        tail_ref[...] = proj()


def _in_proj(x, norm_w, w_cat, *, tm, tn):
    s, d = x.shape
    n_main = MAIN_WIDTH // tn
    n_tail = TAIL_WIDTH // tn
    kern = functools.partial(_in_proj_kernel, n_main=n_main, n_scaled=SB_WIDTH // tn,
                             q_scale=LOG2_E / math.sqrt(SB_HEAD_DIM))
    return pl.pallas_call(
        kern,
        grid=(s // tm, n_main + n_tail),
        in_specs=[
            pl.BlockSpec((tm, d), lambda i, j: (i, 0)),
            pl.BlockSpec((1, d), lambda i, j: (0, 0)),
            pl.BlockSpec((d, tn), lambda i, j: (0, j)),
        ],
        out_specs=[
            pl.BlockSpec((tm, tn), lambda i, j: (i, jnp.minimum(j, n_main - 1))),
            pl.BlockSpec((tm, tn), lambda i, j: (i, jnp.maximum(j - n_main, 0))),
        ],
        out_shape=[jax.ShapeDtypeStruct((s, MAIN_WIDTH), BF16),
                   jax.ShapeDtypeStruct((s, TAIL_WIDTH), F32)],
        scratch_shapes=[pltpu.VMEM((tm, d), BF16)],
        compiler_params=pltpu.CompilerParams(
            dimension_semantics=("arbitrary", "arbitrary"), vmem_limit_bytes=VMEM_LIMIT_BYTES),
        name="in_proj",
    )(x, norm_w, w_cat)


def _rope_table_kernel(pos_ref, freq_ref, cos_ref, sin_ref):
    ang = pos_ref[...].astype(F32) * freq_ref[...]
    cos_ref[...] = jnp.cos(ang)
    sin_ref[...] = jnp.sin(ang)


def _rope_tables(positions):
    s = positions.shape[0]
    per_row = LANES // HALF_ROPE
    inv_freq = ROPE_THETA ** (-jnp.arange(0, MLA_ROPE_DIM, 2, dtype=F32) / MLA_ROPE_DIM)
    pos_rep = jnp.repeat(positions.reshape(s // per_row, per_row), HALF_ROPE, axis=1)
    freq_rep = jnp.tile(inv_freq, per_row).reshape(1, LANES)
    shape = jax.ShapeDtypeStruct((s // per_row, LANES), F32)
    cos, sin = pl.pallas_call(_rope_table_kernel, out_shape=[shape, shape], name="rope_tab")(
        pos_rep, freq_rep)
    zeros = jnp.zeros((s, LANES - MLA_ROPE_DIM), F32)

    def widen(t):
        t = t.reshape(s, HALF_ROPE)
        return jnp.concatenate([t, t, zeros], axis=1)

    return widen(cos), widen(sin)


def _mla_prep_kernel(tail_ref, cos_ref, sin_ref, qnw_ref, kvnw_ref, wq_ref, wqr_ref, wkv_ref,
                     q_ref, k_ref, v_ref, *, q_scale):
    cos = cos_ref[...]
    sin = sin_ref[...]
    c_q = tail_ref[:, 0:Q_LORA_RANK]
    hq = _rms_norm(c_q, qnw_ref[...]).astype(BF16)
    zq = jnp.dot(hq, wq_ref[...], preferred_element_type=F32)
    zr = jnp.dot(hq, wqr_ref[...], preferred_element_type=F32)
    for h in range(MLA_HEADS):
        lo = h * MLA_QK_PAD
        q_ref[:, lo:lo + LANES] = (zq[:, lo:lo + LANES] * q_scale).astype(BF16)
        rot = zq[:, lo + LANES:lo + 2 * LANES] * cos + zr[:, h * LANES:(h + 1) * LANES] * sin
        q_ref[:, lo + LANES:lo + 2 * LANES] = (rot * q_scale).astype(BF16)

    c_kv = tail_ref[:, Q_LORA_RANK:Q_LORA_RANK + KV_LORA_RANK]
    hkv = _rms_norm(c_kv, kvnw_ref[...]).astype(BF16)
    zkv = jnp.dot(hkv, wkv_ref[...], preferred_element_type=F32)
    kr0 = Q_LORA_RANK + KV_LORA_RANK
    k_rot = (tail_ref[:, kr0:kr0 + LANES] * cos + tail_ref[:, kr0 + LANES:kr0 + 2 * LANES] * sin)
    k_rot = k_rot.astype(BF16)
    for h in range(MLA_HEADS):
        lo = h * MLA_QK_PAD
        k_ref[:, lo:lo + LANES] = zkv[:, h * LANES:(h + 1) * LANES].astype(BF16)
        k_ref[:, lo + LANES:lo + 2 * LANES] = k_rot
    v_ref[...] = zkv[:, MLA_HEADS * MLA_NOPE_DIM:].astype(BF16)


def _mla_prep(tail, cos, sin, q_norm_w, kv_norm_w, wq, wqr, wkv, *, tm):
    s = tail.shape[0]
    row = lambda i: (i, 0)
    fixed = lambda i: (0, 0)
    kern = functools.partial(_mla_prep_kernel, q_scale=LOG2_E / math.sqrt(MLA_QK_DIM))
    return pl.pallas_call(
        kern,
        grid=(s // tm,),
        in_specs=[
            pl.BlockSpec((tm, TAIL_WIDTH), row),
            pl.BlockSpec((tm, LANES), row),
            pl.BlockSpec((tm, LANES), row),
            pl.BlockSpec(q_norm_w.shape, fixed),
            pl.BlockSpec(kv_norm_w.shape, fixed),
            pl.BlockSpec(wq.shape, fixed),
            pl.BlockSpec(wqr.shape, fixed),
            pl.BlockSpec(wkv.shape, fixed),
        ],
        out_specs=[
            pl.BlockSpec((tm, MLA_HEADS * MLA_QK_PAD), row),
            pl.BlockSpec((tm, MLA_HEADS * MLA_QK_PAD), row),
            pl.BlockSpec((tm, MLA_WIDTH), row),
        ],
        out_shape=[jax.ShapeDtypeStruct((s, MLA_HEADS * MLA_QK_PAD), BF16),
                   jax.ShapeDtypeStruct((s, MLA_HEADS * MLA_QK_PAD), BF16),
                   jax.ShapeDtypeStruct((s, MLA_WIDTH), BF16)],
        compiler_params=pltpu.CompilerParams(
            dimension_semantics=("arbitrary",), vmem_limit_bytes=VMEM_LIMIT_BYTES),
        name="mla_prep",
    )(tail, cos, sin, q_norm_w, kv_norm_w, wq, wqr, wkv)


def _sb_attn_kernel(q_ref, k_ref, v_ref, g_ref, o_ref, tri_ref, acc_ref, carry_ref, *, tq, tk):
    i = pl.program_id(1)
    q = q_ref[...]

    rows = lax.broadcasted_iota(jnp.int32, (tk, tk + LANES), 0)
    cols = lax.broadcasted_iota(jnp.int32, (tk, tk + LANES), 1)
    tri_ref[...] = jnp.where((rows > cols) | (cols >= tk), -1.0, 0.0).astype(BF16)
    acc_ref[...] = jnp.zeros_like(acc_ref)
    carry_ref[...] = jnp.zeros_like(carry_ref)

    def block(start, masked):
        k = k_ref[pl.ds(start, tk), :]
        v = v_ref[pl.ds(start, tk), :]
        z = _dot_nt(q, k)
        sp = jnp.maximum(z, 0.0) + jnp.log2(1.0 + jnp.exp2(-jnp.abs(z)))
        if masked:
            t_idx = i * tq + lax.broadcasted_iota(jnp.int32, (tq, tk), 0)
            s_idx = start + lax.broadcasted_iota(jnp.int32, (tq, tk), 1)
            before = s_idx < t_idx
            sp = jnp.where(before, sp, 0.0)
        hi = sp.astype(BF16)
        lo = (sp - hi.astype(F32)).astype(BF16)
        tri = tri_ref[...]
        sums = (jnp.dot(hi, tri, preferred_element_type=F32)
                + jnp.dot(lo, tri, preferred_element_type=F32))
        carry = carry_ref[...]
        later = sums[:, :tk] + pltpu.repeat(carry, tk // LANES, axis=1)
        a = jnp.exp2(z - sp + later)
        if masked:
            a = jnp.where(before, a, 0.0)
        acc_ref[...] += jnp.dot(a.astype(BF16), v, preferred_element_type=F32)
        carry_ref[...] = carry + sums[:, tk:]

    n_diag = tq // tk
    for d in reversed(range(n_diag)):
        block(pl.multiple_of(i * tq + d * tk, tk), True)

    def live():
        return (jnp.max(carry_ref[...]) > SB_SKIP_LOG2).astype(jnp.int32)

    def cond(state):
        j, alive = state
        return (j >= 0) & (alive > 0)

    def body(state):
        j, _ = state
        block(pl.multiple_of(j * tk, tk), False)
        return j - 1, live()

    lax.while_loop(cond, body, (i * n_diag - 1, live()))
    o_ref[...] = (acc_ref[...] * _silu(g_ref[...].astype(F32))).astype(o_ref.dtype)


def _sb_attn(main, *, tq, tk):
    s = main.shape[0]
    kern = functools.partial(_sb_attn_kernel, tq=tq, tk=tk)
    return pl.pallas_call(
        kern,
        grid=(SB_HEADS, s // tq),
        in_specs=[
            pl.BlockSpec((tq, SB_HEAD_DIM), lambda h, i: (i, SB_Q_BLK + h)),
            pl.BlockSpec((s, SB_HEAD_DIM), lambda h, i: (0, SB_K_BLK + h)),
            pl.BlockSpec((s, SB_HEAD_DIM), lambda h, i: (0, SB_V_BLK + h)),
            pl.BlockSpec((tq, SB_HEAD_DIM), lambda h, i: (i, SB_G_BLK + h)),
        ],
        out_specs=pl.BlockSpec((tq, SB_HEAD_DIM), lambda h, i: (i, h)),
        out_shape=jax.ShapeDtypeStruct((s, SB_WIDTH), BF16),
        scratch_shapes=[pltpu.VMEM((tk, tk + LANES), BF16),
                        pltpu.VMEM((tq, SB_HEAD_DIM), F32),
                        pltpu.VMEM((tq, LANES), F32)],
        compiler_params=pltpu.CompilerParams(
            dimension_semantics=("arbitrary", "arbitrary"), vmem_limit_bytes=VMEM_LIMIT_BYTES),
        name="sb_attn",
    )(main, main, main, main)


def _mla_attn_kernel(q_ref, k_ref, v_ref, g_ref, o_ref, s_ref, m_ref, l_ref, acc_ref, *, t):
    i = pl.program_id(1)
    q = q_ref[...]
    n_col = t // LANES
    m_ref[...] = jnp.full_like(m_ref, -jnp.inf)
    l_ref[...] = jnp.zeros_like(l_ref)
    acc_ref[...] = jnp.zeros_like(acc_ref)

    def scores(blk, slot):
        start = pl.multiple_of(blk * t, t)
        s_ref[slot] = _dot_nt(q, k_ref[pl.ds(start, t), :])

    def consume(blk, slot, masked):
        start = pl.multiple_of(blk * t, t)

        def column(c):
            s = s_ref[slot, :, c * LANES:(c + 1) * LANES]
            if masked:
                t_chunk = lax.broadcasted_iota(jnp.int32, (t, LANES), 0) // CHUNK
                s_chunk = (c * LANES + lax.broadcasted_iota(jnp.int32, (t, LANES), 1)) // CHUNK
                s = jnp.where(s_chunk <= t_chunk, s, -jnp.inf)
            return s

        col_max = column(0)
        for c in range(1, n_col):
            col_max = jnp.maximum(col_max, column(c))
        m_prev = m_ref[...]
        m_new = jnp.maximum(m_prev, jnp.max(col_max, axis=-1, keepdims=True))
        alpha = jnp.exp2(m_prev - m_new)
        p = [jnp.exp2(column(c) - m_new) for c in range(n_col)]
        p_sum = p[0]
        for c in range(1, n_col):
            p_sum = p_sum + p[c]
        l_ref[...] = alpha * l_ref[...] + p_sum
        p_bf16 = jnp.concatenate([pc.astype(BF16) for pc in p], axis=1)
        acc_ref[...] = alpha * acc_ref[...] + jnp.dot(
            p_bf16, v_ref[pl.ds(start, t), :], preferred_element_type=F32)
        m_ref[...] = m_new

    scores(0, 0)

    def pair(n, _):
        blk = 2 * n
        scores(blk + 1, 1)
        consume(blk, 0, False)
        scores(blk + 2, 0)
        consume(blk + 1, 1, False)
        return 0

    lax.fori_loop(0, lax.shift_right_logical(i, 1), pair, 0)

    @pl.when((i & 1) == 0)
    def _():
        consume(i, 0, True)

    @pl.when((i & 1) == 1)
    def _():
        scores(i, 1)
        consume(i - 1, 0, False)
        consume(i, 1, True)

    l = jnp.sum(l_ref[...], axis=-1, keepdims=True)
    out = acc_ref[...] * (1.0 / l)
    o_ref[...] = (out * _silu(g_ref[...].astype(F32))).astype(o_ref.dtype)


def _mla_attn(q, k, v, main, *, t):
    s = q.shape[0]
    kern = functools.partial(_mla_attn_kernel, t=t)
    return pl.pallas_call(
        kern,
        grid=(MLA_HEADS, s // t),
        in_specs=[
            pl.BlockSpec((t, MLA_QK_PAD), lambda h, i: (i, h)),
            pl.BlockSpec((s, MLA_QK_PAD), lambda h, i: (0, h)),
            pl.BlockSpec((s, MLA_V_DIM), lambda h, i: (0, h)),
            pl.BlockSpec((t, MLA_V_DIM), lambda h, i: (i, MLA_G_BLK + h)),
        ],
        out_specs=pl.BlockSpec((t, MLA_V_DIM), lambda h, i: (i, h)),
        out_shape=jax.ShapeDtypeStruct((s, MLA_WIDTH), BF16),
        scratch_shapes=[pltpu.VMEM((2, t, t), F32),
                        pltpu.VMEM((t, LANES), F32),
                        pltpu.VMEM((t, LANES), F32),
                        pltpu.VMEM((t, MLA_V_DIM), F32)],
        compiler_params=pltpu.CompilerParams(
            dimension_semantics=("arbitrary", "arbitrary"), vmem_limit_bytes=VMEM_LIMIT_BYTES),
        name="mla_attn",
    )(q, k, v, main)


def _out_proj_kernel(a_ref, b_ref, w_ref, x_ref, nw_ref, o_ref):
    y = (jnp.dot(a_ref[...], w_ref[0:SB_WIDTH, :], preferred_element_type=F32)
         + jnp.dot(b_ref[...], w_ref[SB_WIDTH:, :], preferred_element_type=F32))
    o_ref[...] = x_ref[...] + _rms_norm(y, nw_ref[...])


def _out_proj(mix_a, mix_b, w_out, x, norm_w, *, tm):
    s, d = x.shape
    row = lambda i: (i, 0)
    fixed = lambda i: (0, 0)
    return pl.pallas_call(
        _out_proj_kernel,
        grid=(s // tm,),
        in_specs=[
            pl.BlockSpec((tm, SB_WIDTH), row),
            pl.BlockSpec((tm, MLA_WIDTH), row),
            pl.BlockSpec(w_out.shape, fixed),
            pl.BlockSpec((tm, d), row),
            pl.BlockSpec((1, d), fixed),
        ],
        out_specs=pl.BlockSpec((tm, d), row),
        out_shape=jax.ShapeDtypeStruct((s, d), F32),
        compiler_params=pltpu.CompilerParams(
            dimension_semantics=("arbitrary",), vmem_limit_bytes=VMEM_LIMIT_BYTES),
        name="out_proj",
    )(mix_a, mix_b, w_out, x, norm_w)


def _rotate_half_cols(w):
    w1, w2 = jnp.split(w, 2, axis=-1)
    return jnp.concatenate([-w2, w1], axis=-1)


def _prep_in_weight(w_in):
    d = w_in.shape[0]
    splits = np.cumsum([SB_WIDTH] * 4 + [Q_LORA_RANK, KV_LORA_RANK, MLA_ROPE_DIM])
    sb_q, sb_k, sb_v, sb_g, c_q, c_kv, k_rope, mla_g = jnp.split(w_in, splits, axis=1)
    pad = jnp.zeros((d, LANES - MLA_ROPE_DIM), w_in.dtype)
    cat = jnp.concatenate([sb_q, sb_k, sb_v, sb_g, mla_g, c_q, c_kv,
                           k_rope, pad, _rotate_half_cols(k_rope), pad], axis=1)
    return cat.astype(BF16)


def _prep_q_up_weight(w_q_up):
    r = w_q_up.shape[0]
    w = w_q_up.reshape(r, MLA_HEADS, MLA_QK_DIM)
    nope, rope = w[:, :, :MLA_NOPE_DIM], w[:, :, MLA_NOPE_DIM:]
    pad = jnp.zeros((r, MLA_HEADS, LANES - MLA_ROPE_DIM), w.dtype)
    wq = jnp.concatenate([nope, rope, pad], axis=2).reshape(r, MLA_HEADS * MLA_QK_PAD)
    wqr = jnp.concatenate([_rotate_half_cols(rope), pad], axis=2).reshape(r, MLA_HEADS * LANES)
    return wq.astype(BF16), wqr.astype(BF16)


def _prep_kv_up_weight(w_kv_up):
    r = w_kv_up.shape[0]
    w = w_kv_up.reshape(r, MLA_HEADS, MLA_NOPE_DIM + MLA_V_DIM)
    k_nope = w[:, :, :MLA_NOPE_DIM].reshape(r, MLA_HEADS * MLA_NOPE_DIM)
    v = w[:, :, MLA_NOPE_DIM:].reshape(r, MLA_WIDTH)
    return jnp.concatenate([k_nope, v], axis=1).astype(BF16)


def _layer(x, cos, sin, pre_norm_w, w_in, q_norm_w, w_q_up, kv_norm_w, w_kv_up, w_out, post_norm_w):
    main, tail = _in_proj(x, pre_norm_w.reshape(1, -1), _prep_in_weight(w_in), tm=1024, tn=512)
    wq, wqr = _prep_q_up_weight(w_q_up)
    q_mla, k_mla, v_mla = _mla_prep(tail, cos, sin, q_norm_w.reshape(1, -1), kv_norm_w.reshape(1, -1),
                                    wq, wqr, _prep_kv_up_weight(w_kv_up), tm=512)
    mix_a = _sb_attn(main, tq=256, tk=256)
    mix_b = _mla_attn(q_mla, k_mla, v_mla, main, t=512)
    return _out_proj(mix_a, mix_b, w_out.astype(BF16), x, post_norm_w.reshape(1, -1), tm=256)


def kernel(x, positions, pre_norm_w, w_in, q_norm_w, w_q_up, kv_norm_w, w_kv_up, w_out, post_norm_w):
    batch, depth = x.shape[0], pre_norm_w.shape[0]
    outs = []
    for b in range(batch):
        xb = x[b]
        cos, sin = _rope_tables(positions[b])
        for i in range(depth):
            xb = _layer(xb, cos, sin, pre_norm_w[i], w_in[i], q_norm_w[i], w_q_up[i],
                        kv_norm_w[i], w_kv_up[i], w_out[i], post_norm_w[i])
        outs.append(xb)
    return outs[0][None] if batch == 1 else jnp.stack(outs, axis=0)
```

```python
import functools
import math

import numpy as np
import jax
import jax.numpy as jnp
from jax import lax
from jax.experimental import pallas as pl
from jax.experimental.pallas import tpu as pltpu

F32 = jnp.float32
BF16 = jnp.bfloat16

LANES = 128
VMEM_LIMIT_BYTES = 56 * 1024 * 1024

CHUNK = 64
EPS = 1e-6
SB_HEADS = 8
SB_HEAD_DIM = 128
SB_WIDTH = SB_HEADS * SB_HEAD_DIM
MLA_HEADS = 8
MLA_NOPE_DIM = 128
MLA_ROPE_DIM = 64
MLA_QK_DIM = MLA_NOPE_DIM + MLA_ROPE_DIM
MLA_V_DIM = 128
MLA_WIDTH = MLA_HEADS * MLA_V_DIM
Q_LORA_RANK = 512
KV_LORA_RANK = 256
ROPE_THETA = 10000.0
HALF_ROPE = MLA_ROPE_DIM // 2
MLA_QK_PAD = 2 * LANES
LOG2_E = math.log2(math.e)
SB_SKIP_LOG2 = -150.0

MAIN_WIDTH = 4 * SB_WIDTH + MLA_WIDTH
TAIL_WIDTH = Q_LORA_RANK + KV_LORA_RANK + 2 * LANES
MLA_G_BLK = 4 * SB_WIDTH // LANES


def _rms_norm(x, w):
    return x * lax.rsqrt(jnp.mean(x * x, axis=-1, keepdims=True) + EPS) * w


def _silu(g):
    return g * (1.0 / (1.0 + jnp.exp(-g)))


def _dot_nt(a, b):
    return lax.dot_general(a, b, (((1,), (1,)), ((), ())), preferred_element_type=F32)


IN_TILE = 512
N_SB = 4 * SB_WIDTH // IN_TILE
N_MAIN = MAIN_WIDTH // IN_TILE
N_IN_TILES = N_MAIN + TAIL_WIDTH // IN_TILE
REST_WIDTH = MLA_WIDTH + IN_TILE
assert Q_LORA_RANK == IN_TILE and KV_LORA_RANK + 2 * LANES == IN_TILE


def _in_proj_kernel(x_ref, nw_ref, w_ref, wr_ref, main_ref, tail_ref, h_ref, *, q_scale):
    j = pl.program_id(1)

    @pl.when(j == 0)
    def _():
        h_ref[...] = _rms_norm(x_ref[...], nw_ref[...]).astype(BF16)

    def proj(w):
        return jnp.dot(h_ref[...], w, preferred_element_type=F32)

    @pl.when(j < SB_WIDTH // IN_TILE)
    def _():
        main_ref[...] = (proj(w_ref[...].astype(BF16)) * q_scale).astype(BF16)

    @pl.when((j >= SB_WIDTH // IN_TILE) & (j < N_SB))
    def _():
        main_ref[...] = proj(w_ref[...].astype(BF16)).astype(BF16)

    @pl.when((j >= N_SB) & (j < N_MAIN))
    def _():
        main_ref[...] = proj(wr_ref[...]).astype(BF16)

    @pl.when(j == N_MAIN)
    def _():
        tail_ref[...] = proj(w_ref[...].astype(BF16))

    @pl.when(j == N_MAIN + 1)
    def _():
        tail_ref[...] = proj(wr_ref[...])


def _in_proj(x, norm_w, w_in, w_rest, *, tm):
    s, d = x.shape
    n_rest_main = N_MAIN - N_SB
    kern = functools.partial(_in_proj_kernel, q_scale=LOG2_E / math.sqrt(SB_HEAD_DIM))
    w_in_tile = lambda i, j: (0, jnp.where(j < N_SB, j, jnp.where(j < N_MAIN, N_SB - 1, N_SB)))
    w_rest_tile = lambda i, j: (0, jnp.where(j <= N_MAIN, jnp.clip(j - N_SB, 0, n_rest_main - 1),
                                             n_rest_main))
    return pl.pallas_call(
        kern,
        grid=(s // tm, N_IN_TILES),
        in_specs=[
            pl.BlockSpec((tm, d), lambda i, j: (i, 0)),
            pl.BlockSpec((1, d), lambda i, j: (0, 0)),
            pl.BlockSpec((d, IN_TILE), w_in_tile),
            pl.BlockSpec((d, IN_TILE), w_rest_tile),
        ],
        out_specs=[
            pl.BlockSpec((tm, IN_TILE), lambda i, j: (i, jnp.minimum(j, N_MAIN - 1))),
            pl.BlockSpec((tm, IN_TILE), lambda i, j: (i, jnp.maximum(j - N_MAIN, 0))),
        ],
        out_shape=[jax.ShapeDtypeStruct((s, MAIN_WIDTH), BF16),
                   jax.ShapeDtypeStruct((s, TAIL_WIDTH), F32)],
        scratch_shapes=[pltpu.VMEM((tm, d), BF16)],
        compiler_params=pltpu.CompilerParams(
            dimension_semantics=("arbitrary", "arbitrary"), vmem_limit_bytes=VMEM_LIMIT_BYTES),
        name="in_proj",
    )(x, norm_w, w_in, w_rest)


def _rope_table_kernel(pos_ref, freq_ref, cos_ref, sin_ref):
    ang = pos_ref[...].astype(F32) * freq_ref[...]
    cos_ref[...] = jnp.cos(ang)
    sin_ref[...] = jnp.sin(ang)


def _rope_tables(positions):
    s = positions.shape[0]
    per_row = LANES // HALF_ROPE
    inv_freq = ROPE_THETA ** (-jnp.arange(0, MLA_ROPE_DIM, 2, dtype=F32) / MLA_ROPE_DIM)
    pos_rep = jnp.repeat(positions.reshape(s // per_row, per_row), HALF_ROPE, axis=1)
    freq_rep = jnp.tile(inv_freq, per_row).reshape(1, LANES)
    shape = jax.ShapeDtypeStruct((s // per_row, LANES), F32)
    cos, sin = pl.pallas_call(_rope_table_kernel, out_shape=[shape, shape], name="rope_tab")(
        pos_rep, freq_rep)
    zeros = jnp.zeros((s, LANES - MLA_ROPE_DIM), F32)

    def widen(t):
        t = t.reshape(s, HALF_ROPE)
        return jnp.concatenate([t, t, zeros], axis=1)

    return widen(cos), widen(sin)


def _mla_prep_kernel(tail_ref, cos_ref, sin_ref, qnw_ref, kvnw_ref, wq_ref, wqr_ref, wkv_ref,
                     q_ref, k_ref, v_ref, *, q_scale):
    cos = cos_ref[...]
    sin = sin_ref[...]
    c_q = tail_ref[:, 0:Q_LORA_RANK]
    hq = _rms_norm(c_q, qnw_ref[...]).astype(BF16)
    zq = jnp.dot(hq, wq_ref[...], preferred_element_type=F32)
    zr = jnp.dot(hq, wqr_ref[...], preferred_element_type=F32)
    for h in range(MLA_HEADS):
        lo = h * MLA_QK_PAD
        q_ref[:, lo:lo + LANES] = (zq[:, lo:lo + LANES] * q_scale).astype(BF16)
        rot = zq[:, lo + LANES:lo + 2 * LANES] * cos + zr[:, h * LANES:(h + 1) * LANES] * sin
        q_ref[:, lo + LANES:lo + 2 * LANES] = (rot * q_scale).astype(BF16)

    c_kv = tail_ref[:, Q_LORA_RANK:Q_LORA_RANK + KV_LORA_RANK]
    hkv = _rms_norm(c_kv, kvnw_ref[...]).astype(BF16)
    zkv = jnp.dot(hkv, wkv_ref[...], preferred_element_type=F32)
    kr0 = Q_LORA_RANK + KV_LORA_RANK
    k_rot = (tail_ref[:, kr0:kr0 + LANES] * cos + tail_ref[:, kr0 + LANES:kr0 + 2 * LANES] * sin)
    k_rot = k_rot.astype(BF16)
    for h in range(MLA_HEADS):
        lo = h * MLA_QK_PAD
        k_ref[:, lo:lo + LANES] = zkv[:, h * LANES:(h + 1) * LANES].astype(BF16)
        k_ref[:, lo + LANES:lo + 2 * LANES] = k_rot
    v_ref[...] = zkv[:, MLA_HEADS * MLA_NOPE_DIM:].astype(BF16)


def _mla_prep(tail, cos, sin, q_norm_w, kv_norm_w, wq, wqr, wkv, *, tm):
    s = tail.shape[0]
    row = lambda i: (i, 0)
    fixed = lambda i: (0, 0)
    kern = functools.partial(_mla_prep_kernel, q_scale=LOG2_E / math.sqrt(MLA_QK_DIM))
    return pl.pallas_call(
        kern,
        grid=(s // tm,),
        in_specs=[
            pl.BlockSpec((tm, TAIL_WIDTH), row),
            pl.BlockSpec((tm, LANES), row),
            pl.BlockSpec((tm, LANES), row),
            pl.BlockSpec(q_norm_w.shape, fixed),
            pl.BlockSpec(kv_norm_w.shape, fixed),
            pl.BlockSpec(wq.shape, fixed),
            pl.BlockSpec(wqr.shape, fixed),
            pl.BlockSpec(wkv.shape, fixed),
        ],
        out_specs=[
            pl.BlockSpec((tm, MLA_HEADS * MLA_QK_PAD), row),
            pl.BlockSpec((tm, MLA_HEADS * MLA_QK_PAD), row),
            pl.BlockSpec((tm, MLA_WIDTH), row),
        ],
        out_shape=[jax.ShapeDtypeStruct((s, MLA_HEADS * MLA_QK_PAD), BF16),
                   jax.ShapeDtypeStruct((s, MLA_HEADS * MLA_QK_PAD), BF16),
                   jax.ShapeDtypeStruct((s, MLA_WIDTH), BF16)],
        compiler_params=pltpu.CompilerParams(
            dimension_semantics=("arbitrary",), vmem_limit_bytes=VMEM_LIMIT_BYTES),
        name="mla_prep",
    )(tail, cos, sin, q_norm_w, kv_norm_w, wq, wqr, wkv)


def _sb_attn_kernel(q_ref, k_ref, v_ref, g_ref, o_ref, tri_ref, acc_ref, carry_ref, *, tq, tk):
    i = pl.program_id(1)
    n_heads = q_ref.shape[1] // SB_HEAD_DIM

    rows = lax.broadcasted_iota(jnp.int32, (tk, tk), 0)
    cols = lax.broadcasted_iota(jnp.int32, (tk, tk), 1)
    tri = jnp.where(rows >= cols, -1.0, 0.0).astype(BF16)
    tri_ref[0:tk, :] = tri
    tri_ref[tk:2 * tk, :] = tri
    acc_ref[...] = jnp.zeros_like(acc_ref)
    carry_ref[...] = jnp.zeros_like(carry_ref)

    def block(start, masked):
        heads = range(n_heads)
        lanes = [slice(h * SB_HEAD_DIM, (h + 1) * SB_HEAD_DIM) for h in heads]
        z = [_dot_nt(q_ref[:, lanes[h]], k_ref[pl.ds(start, tk), lanes[h]]) for h in heads]
        sp = [jnp.maximum(z[h], 0.0) + jnp.log2(1.0 + jnp.exp2(-jnp.abs(z[h]))) for h in heads]
        if masked:
            t_idx = i * tq + lax.broadcasted_iota(jnp.int32, (tq, tk), 0)
            s_idx = start + lax.broadcasted_iota(jnp.int32, (tq, tk), 1)
            before = s_idx < t_idx
            sp = [jnp.where(before, sp[h], 0.0) for h in heads]
        tri = tri_ref[...]
        sums = []
        for h in heads:
            hi = sp[h].astype(BF16)
            lo = (sp[h] - hi.astype(F32)).astype(BF16)
            sums.append(jnp.dot(jnp.concatenate([hi, lo], axis=1), tri,
                                preferred_element_type=F32))
        carry = [carry_ref[:, lanes[h]] for h in heads]
        a = [jnp.exp2(z[h] + sums[h] + jnp.concatenate([carry[h]] * (tk // LANES), axis=1))
             for h in heads]
        if masked:
            a = [jnp.where(before, a[h], 0.0) for h in heads]
        for h in heads:
            acc_ref[:, lanes[h]] += jnp.dot(a[h].astype(BF16), v_ref[pl.ds(start, tk), lanes[h]],
                                            preferred_element_type=F32)
            carry_ref[:, lanes[h]] = carry[h] + jnp.broadcast_to(sums[h][:, 0:1], (tq, LANES))

    n_diag = tq // tk
    for d in reversed(range(n_diag)):
        block(pl.multiple_of(i * tq + d * tk, tk), True)

    def live():
        return (jnp.max(carry_ref[...]) > SB_SKIP_LOG2).astype(jnp.int32)

    def cond(state):
        j, alive = state
        return (j >= 0) & (alive > 0)

    def body(state):
        j, _ = state
        block(pl.multiple_of(j * tk, tk), False)
        return j - 1, live()

    lax.while_loop(cond, body, (i * n_diag - 1, live()))
    o_ref[...] = (acc_ref[...] * _silu(g_ref[...].astype(F32))).astype(o_ref.dtype)


def _sb_attn(main, *, tq, tk, heads_per_step):
    s = main.shape[0]
    n_groups = SB_HEADS // heads_per_step
    width = heads_per_step * SB_HEAD_DIM
    kern = functools.partial(_sb_attn_kernel, tq=tq, tk=tk)
    return pl.pallas_call(
        kern,
        grid=(n_groups, s // tq),
        in_specs=[
            pl.BlockSpec((tq, width), lambda g, i: (i, g)),
            pl.BlockSpec((s, width), lambda g, i: (0, n_groups + g)),
            pl.BlockSpec((s, width), lambda g, i: (0, 2 * n_groups + g)),
            pl.BlockSpec((tq, width), lambda g, i: (i, 3 * n_groups + g)),
        ],
        out_specs=pl.BlockSpec((tq, width), lambda g, i: (i, g)),
        out_shape=jax.ShapeDtypeStruct((s, SB_WIDTH), BF16),
        scratch_shapes=[pltpu.VMEM((2 * tk, tk), BF16),
                        pltpu.VMEM((tq, width), F32),
                        pltpu.VMEM((tq, width), F32)],
        compiler_params=pltpu.CompilerParams(
            dimension_semantics=("arbitrary", "arbitrary"), vmem_limit_bytes=VMEM_LIMIT_BYTES),
        name="sb_attn",
    )(main, main, main, main)


def _mla_attn_kernel(q_ref, k_ref, v_ref, g_ref, o_ref, s_ref, m_ref, acc_ref, *, tk):
    i = pl.program_id(1)
    tq = 2 * tk
    n_col = tk // LANES
    row_tile = 256
    top, bottom, both = slice(0, tk), slice(tk, tq), slice(0, tq)
    ones = jnp.ones((tk, LANES), BF16)
    m_ref[...] = jnp.full_like(m_ref, -jnp.inf)
    acc_ref[...] = jnp.zeros_like(acc_ref)

    def scores(blk, slot, rows):
        start = pl.multiple_of(blk * tk, tk)
        s_ref[slot, rows] = _dot_nt(q_ref[rows, :], k_ref[pl.ds(start, tk), :])

    def consume(blk, slot, rows, masked, next_blk=None):
        start = pl.multiple_of(blk * tk, tk)
        for r0 in range(rows.start, rows.stop, row_tile):
            tile = slice(r0, r0 + row_tile)
            if next_blk is not None:
                scores(next_blk, 1 - slot, tile)
            consume_rows(start, slot, tile, r0 - rows.start, masked)

    def consume_rows(start, slot, rows, row_offset, masked):
        def column(c):
            s = s_ref[slot, rows, c * LANES:(c + 1) * LANES]
            if masked:
                t_chunk = (row_offset + lax.broadcasted_iota(jnp.int32, (row_tile, LANES), 0)) // CHUNK
                s_chunk = (c * LANES + lax.broadcasted_iota(jnp.int32, (row_tile, LANES), 1)) // CHUNK
                s = jnp.where(s_chunk <= t_chunk, s, -jnp.inf)
            return s

        col_max = column(0)
        for c in range(1, n_col):
            col_max = jnp.maximum(col_max, column(c))
        m_prev = m_ref[rows]
        m_new = jnp.maximum(m_prev, jnp.max(col_max, axis=-1, keepdims=True))
        alpha = jnp.exp2(m_prev - m_new)
        p = jnp.concatenate([jnp.exp2((column(c) - m_new).astype(BF16)) for c in range(n_col)],
                            axis=1)
        v_ones = jnp.concatenate([v_ref[pl.ds(start, tk), :], ones], axis=1)
        acc_ref[rows] = (jnp.concatenate([alpha, alpha], axis=1) * acc_ref[rows]
                         + jnp.dot(p, v_ones, preferred_element_type=F32))
        m_ref[rows] = m_new

    scores(0, 0, both)

    def pair(n, _):
        blk = 2 * n
        consume(blk, 0, both, False, next_blk=blk + 1)
        consume(blk + 1, 1, both, False, next_blk=blk + 2)
        return 0

    lax.fori_loop(0, i, pair, 0)

    scores(2 * i + 1, 1, bottom)
    consume(2 * i, 0, top, True)
    consume(2 * i, 0, bottom, False)
    consume(2 * i + 1, 1, bottom, True)

    out = acc_ref[:, 0:MLA_V_DIM] * (1.0 / acc_ref[:, MLA_V_DIM:])
    o_ref[...] = (out * _silu(g_ref[...].astype(F32))).astype(o_ref.dtype)


def _mla_attn(q, k, v, main, *, tk):
    s = q.shape[0]
    tq = 2 * tk
    kern = functools.partial(_mla_attn_kernel, tk=tk)
    return pl.pallas_call(
        kern,
        grid=(MLA_HEADS, s // tq),
        in_specs=[
            pl.BlockSpec((tq, MLA_QK_PAD), lambda h, i: (i, h)),
            pl.BlockSpec((s, MLA_QK_PAD), lambda h, i: (0, h)),
            pl.BlockSpec((s, MLA_V_DIM), lambda h, i: (0, h)),
            pl.BlockSpec((tq, MLA_V_DIM), lambda h, i: (i, MLA_G_BLK + h)),
        ],
        out_specs=pl.BlockSpec((tq, MLA_V_DIM), lambda h, i: (i, h)),
        out_shape=jax.ShapeDtypeStruct((s, MLA_WIDTH), BF16),
        scratch_shapes=[pltpu.VMEM((2, tq, tk), F32),
                        pltpu.VMEM((tq, LANES), F32),
                        pltpu.VMEM((tq, MLA_V_DIM + LANES), F32)],
        compiler_params=pltpu.CompilerParams(
            dimension_semantics=("arbitrary", "arbitrary"), vmem_limit_bytes=VMEM_LIMIT_BYTES),
        name="mla_attn",
    )(q, k, v, main)


def _out_proj_kernel(a_ref, b_ref, w_ref, x_ref, nw_ref, o_ref):
    y = (jnp.dot(a_ref[...], w_ref[0:SB_WIDTH, :], preferred_element_type=F32)
         + jnp.dot(b_ref[...], w_ref[SB_WIDTH:, :], preferred_element_type=F32))
    o_ref[...] = x_ref[...] + _rms_norm(y, nw_ref[...])


def _out_proj(mix_a, mix_b, w_out, x, norm_w, *, tm):
    s, d = x.shape
    row = lambda i: (i, 0)
    fixed = lambda i: (0, 0)
    return pl.pallas_call(
        _out_proj_kernel,
        grid=(s // tm,),
        in_specs=[
            pl.BlockSpec((tm, SB_WIDTH), row),
            pl.BlockSpec((tm, MLA_WIDTH), row),
            pl.BlockSpec(w_out.shape, fixed),
            pl.BlockSpec((tm, d), row),
            pl.BlockSpec((1, d), fixed),
        ],
        out_specs=pl.BlockSpec((tm, d), row),
        out_shape=jax.ShapeDtypeStruct((s, d), F32),
        compiler_params=pltpu.CompilerParams(
            dimension_semantics=("arbitrary",), vmem_limit_bytes=VMEM_LIMIT_BYTES),
        name="out_proj",
    )(mix_a, mix_b, w_out, x, norm_w)


def _rotate_half_cols(w):
    w1, w2 = jnp.split(w, 2, axis=-1)
    return jnp.concatenate([-w2, w1], axis=-1)


def _prep_in_weight(w_in):
    d = w_in.shape[0]
    lo = 4 * SB_WIDTH + Q_LORA_RANK
    w = w_in[:, lo:].astype(BF16)
    c_kv, k_rope, mla_g = jnp.split(w, [KV_LORA_RANK, KV_LORA_RANK + MLA_ROPE_DIM], axis=1)
    pad = jnp.zeros((d, LANES - MLA_ROPE_DIM), BF16)
    return jnp.concatenate([mla_g, c_kv, k_rope, pad, _rotate_half_cols(k_rope), pad], axis=1)


def _prep_q_up_weight(w_q_up):
    r = w_q_up.shape[0]
    w = w_q_up.reshape(r, MLA_HEADS, MLA_QK_DIM)
    nope, rope = w[:, :, :MLA_NOPE_DIM], w[:, :, MLA_NOPE_DIM:]
    pad = jnp.zeros((r, MLA_HEADS, LANES - MLA_ROPE_DIM), w.dtype)
    wq = jnp.concatenate([nope, rope, pad], axis=2).reshape(r, MLA_HEADS * MLA_QK_PAD)
    wqr = jnp.concatenate([_rotate_half_cols(rope), pad], axis=2).reshape(r, MLA_HEADS * LANES)
    return wq.astype(BF16), wqr.astype(BF16)


def _prep_kv_up_weight(w_kv_up):
    r = w_kv_up.shape[0]
    w = w_kv_up.reshape(r, MLA_HEADS, MLA_NOPE_DIM + MLA_V_DIM)
    k_nope = w[:, :, :MLA_NOPE_DIM].reshape(r, MLA_HEADS * MLA_NOPE_DIM)
    v = w[:, :, MLA_NOPE_DIM:].reshape(r, MLA_WIDTH)
    return jnp.concatenate([k_nope, v], axis=1).astype(BF16)


def _layer(x, cos, sin, pre_norm_w, w_in, q_norm_w, w_q_up, kv_norm_w, w_kv_up, w_out, post_norm_w):
    main, tail = _in_proj(x, pre_norm_w.reshape(1, -1), w_in, _prep_in_weight(w_in), tm=1024)
    wq, wqr = _prep_q_up_weight(w_q_up)
    q_mla, k_mla, v_mla = _mla_prep(tail, cos, sin, q_norm_w.reshape(1, -1), kv_norm_w.reshape(1, -1),
                                    wq, wqr, _prep_kv_up_weight(w_kv_up), tm=512)
    mix_a = _sb_attn(main, tq=256, tk=256, heads_per_step=4)
    mix_b = _mla_attn(q_mla, k_mla, v_mla, main, tk=512)
    return _out_proj(mix_a, mix_b, w_out.astype(BF16), x, post_norm_w.reshape(1, -1), tm=256)


def kernel(x, positions, pre_norm_w, w_in, q_norm_w, w_q_up, kv_norm_w, w_kv_up, w_out, post_norm_w):
    batch, depth = x.shape[0], pre_norm_w.shape[0]
    outs = []
    for b in range(batch):
        xb = x[b]
        cos, sin = _rope_tables(positions[b])
        for i in range(depth):
            xb = _layer(xb, cos, sin, pre_norm_w[i], w_in[i], q_norm_w[i], w_q_up[i],
                        kv_norm_w[i], w_kv_up[i], w_out[i], post_norm_w[i])
        outs.append(xb)
    return outs[0][None] if batch == 1 else jnp.stack(outs, axis=0)
```

```python
import functools
import math

import numpy as np
import jax
import jax.numpy as jnp
from jax import lax
from jax.experimental import pallas as pl
from jax.experimental.pallas import tpu as pltpu

F32 = jnp.float32
BF16 = jnp.bfloat16

LANES = 128
VMEM_LIMIT_BYTES = 56 * 1024 * 1024

CHUNK = 64
EPS = 1e-6
SB_HEADS = 8
SB_HEAD_DIM = 128
SB_WIDTH = SB_HEADS * SB_HEAD_DIM
MLA_HEADS = 8
MLA_NOPE_DIM = 128
MLA_ROPE_DIM = 64
MLA_QK_DIM = MLA_NOPE_DIM + MLA_ROPE_DIM
MLA_V_DIM = 128
MLA_WIDTH = MLA_HEADS * MLA_V_DIM
Q_LORA_RANK = 512
KV_LORA_RANK = 256
ROPE_THETA = 10000.0
HALF_ROPE = MLA_ROPE_DIM // 2
MLA_QK_PAD = 2 * LANES
LOG2_E = math.log2(math.e)
SB_SKIP_LOG2 = -150.0

MAIN_WIDTH = 4 * SB_WIDTH + MLA_WIDTH
TAIL_WIDTH = Q_LORA_RANK + KV_LORA_RANK + 2 * LANES
MLA_G_BLK = 4 * SB_WIDTH // LANES


def _rms_norm(x, w):
    return x * lax.rsqrt(jnp.mean(x * x, axis=-1, keepdims=True) + EPS) * w


def _silu(g):
    return g * (1.0 / (1.0 + jnp.exp(-g)))


def _dot_nt(a, b):
    return lax.dot_general(a, b, (((1,), (1,)), ((), ())), preferred_element_type=F32)


IN_TILE = 512
N_SB = 4 * SB_WIDTH // IN_TILE
N_MAIN = MAIN_WIDTH // IN_TILE
N_IN_TILES = N_MAIN + TAIL_WIDTH // IN_TILE
C_Q_ROW = 4 * SB_WIDTH
C_KV_ROW = C_Q_ROW + Q_LORA_RANK
MLA_GATE_ROW = C_KV_ROW + KV_LORA_RANK + MLA_ROPE_DIM
assert Q_LORA_RANK == IN_TILE and MLA_WIDTH == 2 * IN_TILE


def _in_proj_kernel(x_ref, nw_ref, wt_ref, main_ref, tail_ref, h_ref, *, q_scale):
    j = pl.program_id(1)

    @pl.when(j == 0)
    def _():
        h_ref[...] = _rms_norm(x_ref[...], nw_ref[...]).astype(BF16)

    def proj():
        return _dot_nt(h_ref[...], wt_ref[...].astype(BF16))

    @pl.when(j < SB_WIDTH // IN_TILE)
    def _():
        main_ref[...] = (proj() * q_scale).astype(BF16)

    @pl.when((j >= SB_WIDTH // IN_TILE) & (j < N_MAIN))
    def _():
        main_ref[...] = proj().astype(BF16)

    @pl.when(j >= N_MAIN)
    def _():
        tail_ref[...] = proj()


def _in_proj(x, norm_w, w_in_t, *, tm):
    s, d = x.shape
    kern = functools.partial(_in_proj_kernel, q_scale=LOG2_E / math.sqrt(SB_HEAD_DIM))

    def weight_row(i, j):
        unit = MLA_ROPE_DIM
        step = IN_TILE // unit
        row = jnp.where(j < N_SB, j * step,
                        jnp.where(j < N_MAIN, MLA_GATE_ROW // unit + (j - N_SB) * step,
                                  jnp.where(j == N_MAIN, C_Q_ROW // unit, C_KV_ROW // unit)))
        return (row * unit, 0)

    return pl.pallas_call(
        kern,
        grid=(s // tm, N_IN_TILES),
        in_specs=[
            pl.BlockSpec((tm, d), lambda i, j: (i, 0)),
            pl.BlockSpec((1, d), lambda i, j: (0, 0)),
            pl.BlockSpec((pl.Element(IN_TILE), pl.Element(d)), weight_row),
        ],
        out_specs=[
            pl.BlockSpec((tm, IN_TILE), lambda i, j: (i, jnp.minimum(j, N_MAIN - 1))),
            pl.BlockSpec((tm, IN_TILE), lambda i, j: (i, jnp.maximum(j - N_MAIN, 0))),
        ],
        out_shape=[jax.ShapeDtypeStruct((s, MAIN_WIDTH), BF16),
                   jax.ShapeDtypeStruct((s, TAIL_WIDTH), F32)],
        scratch_shapes=[pltpu.VMEM((tm, d), BF16)],
        compiler_params=pltpu.CompilerParams(
            dimension_semantics=("arbitrary", "arbitrary"), vmem_limit_bytes=VMEM_LIMIT_BYTES),
        name="in_proj",
    )(x, norm_w, w_in_t)


def _rope_table_kernel(pos_ref, freq_ref, cos_ref, sin_ref):
    ang = pos_ref[...].astype(F32) * freq_ref[...]
    cos_ref[...] = jnp.cos(ang)
    sin_ref[...] = jnp.sin(ang)


def _rope_tables(positions):
    s = positions.shape[0]
    per_row = LANES // HALF_ROPE
    inv_freq = ROPE_THETA ** (-jnp.arange(0, MLA_ROPE_DIM, 2, dtype=F32) / MLA_ROPE_DIM)
    pos_rep = jnp.repeat(positions.reshape(s // per_row, per_row), HALF_ROPE, axis=1)
    freq_rep = jnp.tile(inv_freq, per_row).reshape(1, LANES)
    shape = jax.ShapeDtypeStruct((s // per_row, LANES), F32)
    cos, sin = pl.pallas_call(_rope_table_kernel, out_shape=[shape, shape], name="rope_tab")(
        pos_rep, freq_rep)
    zeros = jnp.zeros((s, LANES - MLA_ROPE_DIM), F32)

    def widen(t):
        t = t.reshape(s, HALF_ROPE)
        return jnp.concatenate([t, t, zeros], axis=1)

    return widen(cos), widen(sin)


def _mla_prep_kernel(tail_ref, cos_ref, sin_ref, qnw_ref, kvnw_ref, wq_ref, wqr_ref, wkv_ref,
                     q_ref, k_ref, v_ref, *, q_scale):
    cos = cos_ref[...]
    sin = sin_ref[...]
    c_q = tail_ref[:, 0:Q_LORA_RANK]
    hq = _rms_norm(c_q, qnw_ref[...]).astype(BF16)
    zq = jnp.dot(hq, wq_ref[...], preferred_element_type=F32)
    zr = jnp.dot(hq, wqr_ref[...], preferred_element_type=F32)
    for h in range(MLA_HEADS):
        lo = h * MLA_QK_PAD
        q_ref[:, lo:lo + LANES] = (zq[:, lo:lo + LANES] * q_scale).astype(BF16)
        rot = zq[:, lo + LANES:lo + 2 * LANES] * cos + zr[:, h * LANES:(h + 1) * LANES] * sin
        q_ref[:, lo + LANES:lo + 2 * LANES] = (rot * q_scale).astype(BF16)

    c_kv = tail_ref[:, Q_LORA_RANK:Q_LORA_RANK + KV_LORA_RANK]
    hkv = _rms_norm(c_kv, kvnw_ref[...]).astype(BF16)
    zkv = jnp.dot(hkv, wkv_ref[...], preferred_element_type=F32)
    kr0 = Q_LORA_RANK + KV_LORA_RANK
    kr = tail_ref[:, kr0:kr0 + LANES]
    lane = lax.broadcasted_iota(jnp.int32, kr.shape, 1)
    rotated = jnp.where(lane < HALF_ROPE,
                        -pltpu.roll(kr, LANES - HALF_ROPE, axis=1),
                        pltpu.roll(kr, HALF_ROPE, axis=1))
    k_rot = (kr * cos + rotated * sin).astype(BF16)
    for h in range(MLA_HEADS):
        lo = h * MLA_QK_PAD
        k_ref[:, lo:lo + LANES] = zkv[:, h * LANES:(h + 1) * LANES].astype(BF16)
        k_ref[:, lo + LANES:lo + 2 * LANES] = k_rot
    v_ref[...] = zkv[:, MLA_HEADS * MLA_NOPE_DIM:].astype(BF16)


def _mla_prep(tail, cos, sin, q_norm_w, kv_norm_w, wq, wqr, wkv, *, tm):
    s = tail.shape[0]
    row = lambda i: (i, 0)
    fixed = lambda i: (0, 0)
    kern = functools.partial(_mla_prep_kernel, q_scale=LOG2_E / math.sqrt(MLA_QK_DIM))
    return pl.pallas_call(
        kern,
        grid=(s // tm,),
        in_specs=[
            pl.BlockSpec((tm, TAIL_WIDTH), row),
            pl.BlockSpec((tm, LANES), row),
            pl.BlockSpec((tm, LANES), row),
            pl.BlockSpec(q_norm_w.shape, fixed),
            pl.BlockSpec(kv_norm_w.shape, fixed),
            pl.BlockSpec(wq.shape, fixed),
            pl.BlockSpec(wqr.shape, fixed),
            pl.BlockSpec(wkv.shape, fixed),
        ],
        out_specs=[
            pl.BlockSpec((tm, MLA_HEADS * MLA_QK_PAD), row),
            pl.BlockSpec((tm, MLA_HEADS * MLA_QK_PAD), row),
            pl.BlockSpec((tm, MLA_WIDTH), row),
        ],
        out_shape=[jax.ShapeDtypeStruct((s, MLA_HEADS * MLA_QK_PAD), BF16),
                   jax.ShapeDtypeStruct((s, MLA_HEADS * MLA_QK_PAD), BF16),
                   jax.ShapeDtypeStruct((s, MLA_WIDTH), BF16)],
        compiler_params=pltpu.CompilerParams(
            dimension_semantics=("arbitrary",), vmem_limit_bytes=VMEM_LIMIT_BYTES),
        name="mla_prep",
    )(tail, cos, sin, q_norm_w, kv_norm_w, wq, wqr, wkv)


def _sb_attn_kernel(q_ref, k_ref, v_ref, g_ref, o_ref, tri_ref, acc_ref, carry_ref, *, tq, tk):
    i = pl.program_id(1)
    n_heads = q_ref.shape[1] // SB_HEAD_DIM

    rows = lax.broadcasted_iota(jnp.int32, (tk, tk), 0)
    cols = lax.broadcasted_iota(jnp.int32, (tk, tk), 1)
    tri = jnp.where(rows >= cols, -1.0, 0.0).astype(BF16)
    tri_ref[0:tk, :] = tri
    tri_ref[tk:2 * tk, :] = tri
    acc_ref[...] = jnp.zeros_like(acc_ref)
    carry_ref[...] = jnp.zeros_like(carry_ref)

    def block(start, masked):
        heads = range(n_heads)
        lanes = [slice(h * SB_HEAD_DIM, (h + 1) * SB_HEAD_DIM) for h in heads]
        z = [_dot_nt(q_ref[:, lanes[h]], k_ref[pl.ds(start, tk), lanes[h]]) for h in heads]
        sp = [jnp.maximum(z[h], 0.0) + jnp.log2(1.0 + jnp.exp2(-jnp.abs(z[h]))) for h in heads]
        if masked:
            t_idx = i * tq + lax.broadcasted_iota(jnp.int32, (tq, tk), 0)
            s_idx = start + lax.broadcasted_iota(jnp.int32, (tq, tk), 1)
            before = s_idx < t_idx
            sp = [jnp.where(before, sp[h], 0.0) for h in heads]
        tri = tri_ref[...]
        sums = []
        for h in heads:
            hi = sp[h].astype(BF16)
            lo = (sp[h] - hi.astype(F32)).astype(BF16)
            sums.append(jnp.dot(jnp.concatenate([hi, lo], axis=1), tri,
                                preferred_element_type=F32))
        carry = [carry_ref[:, lanes[h]] for h in heads]
        a = [jnp.exp2(z[h] + sums[h] + jnp.concatenate([carry[h]] * (tk // LANES), axis=1))
             for h in heads]
        if masked:
            a = [jnp.where(before, a[h], 0.0) for h in heads]
        for h in heads:
            acc_ref[:, lanes[h]] += jnp.dot(a[h].astype(BF16), v_ref[pl.ds(start, tk), lanes[h]],
                                            preferred_element_type=F32)
            carry_ref[:, lanes[h]] = carry[h] + jnp.broadcast_to(sums[h][:, 0:1], (tq, LANES))

    n_diag = tq // tk
    for d in reversed(range(n_diag)):
        block(pl.multiple_of(i * tq + d * tk, tk), True)

    def live():
        return (jnp.max(carry_ref[...]) > SB_SKIP_LOG2).astype(jnp.int32)

    def cond(state):
        j, alive = state
        return (j >= 0) & (alive > 0)

    def body(state):
        j, _ = state
        block(pl.multiple_of(j * tk, tk), False)
        return j - 1, live()

    lax.while_loop(cond, body, (i * n_diag - 1, live()))
    o_ref[...] = (acc_ref[...] * _silu(g_ref[...].astype(F32))).astype(o_ref.dtype)


def _sb_attn(main, *, tq, tk, heads_per_step):
    s = main.shape[0]
    n_groups = SB_HEADS // heads_per_step
    width = heads_per_step * SB_HEAD_DIM
    kern = functools.partial(_sb_attn_kernel, tq=tq, tk=tk)
    return pl.pallas_call(
        kern,
        grid=(n_groups, s // tq),
        in_specs=[
            pl.BlockSpec((tq, width), lambda g, i: (i, g)),
            pl.BlockSpec((s, width), lambda g, i: (0, n_groups + g)),
            pl.BlockSpec((s, width), lambda g, i: (0, 2 * n_groups + g)),
            pl.BlockSpec((tq, width), lambda g, i: (i, 3 * n_groups + g)),
        ],
        out_specs=pl.BlockSpec((tq, width), lambda g, i: (i, g)),
        out_shape=jax.ShapeDtypeStruct((s, SB_WIDTH), BF16),
        scratch_shapes=[pltpu.VMEM((2 * tk, tk), BF16),
                        pltpu.VMEM((tq, width), F32),
                        pltpu.VMEM((tq, width), F32)],
        compiler_params=pltpu.CompilerParams(
            dimension_semantics=("arbitrary", "arbitrary"), vmem_limit_bytes=VMEM_LIMIT_BYTES),
        name="sb_attn",
    )(main, main, main, main)


def _mla_attn_kernel(q_ref, k_ref, v_ref, g_ref, o_ref, s_ref, m_ref, acc_ref, *, tk):
    i = pl.program_id(1)
    tq = 2 * tk
    n_col = tk // LANES
    row_tile = 256
    top, bottom, both = slice(0, tk), slice(tk, tq), slice(0, tq)
    ones = jnp.ones((tk, LANES), BF16)
    m_ref[...] = jnp.full_like(m_ref, -jnp.inf)
    acc_ref[...] = jnp.zeros_like(acc_ref)

    def scores(blk, slot, rows):
        start = pl.multiple_of(blk * tk, tk)
        s_ref[slot, rows] = _dot_nt(q_ref[rows, :], k_ref[pl.ds(start, tk), :])

    def consume(blk, slot, rows, masked, next_blk=None):
        start = pl.multiple_of(blk * tk, tk)
        for r0 in range(rows.start, rows.stop, row_tile):
            tile = slice(r0, r0 + row_tile)
            if next_blk is not None:
                scores(next_blk, 1 - slot, tile)
            consume_rows(start, slot, tile, r0 - rows.start, masked)

    def consume_rows(start, slot, rows, row_offset, masked):
        def column(c):
            s = s_ref[slot, rows, c * LANES:(c + 1) * LANES]
            if masked:
                t_chunk = (row_offset + lax.broadcasted_iota(jnp.int32, (row_tile, LANES), 0)) // CHUNK
                s_chunk = (c * LANES + lax.broadcasted_iota(jnp.int32, (row_tile, LANES), 1)) // CHUNK
                s = jnp.where(s_chunk <= t_chunk, s, -jnp.inf)
            return s

        col_max = column(0)
        for c in range(1, n_col):
            col_max = jnp.maximum(col_max, column(c))
        m_prev = m_ref[rows]
        m_new = jnp.maximum(m_prev, jnp.max(col_max, axis=-1, keepdims=True))
        alpha = jnp.exp2(m_prev - m_new)
        p = jnp.concatenate([jnp.exp2((column(c) - m_new).astype(BF16)) for c in range(n_col)],
                            axis=1)
        v_ones = jnp.concatenate([v_ref[pl.ds(start, tk), :], ones], axis=1)
        acc_ref[rows] = (jnp.concatenate([alpha, alpha], axis=1) * acc_ref[rows]
                         + jnp.dot(p, v_ones, preferred_element_type=F32))
        m_ref[rows] = m_new

    scores(0, 0, both)

    def pair(n, _):
        blk = 2 * n
        consume(blk, 0, both, False, next_blk=blk + 1)
        consume(blk + 1, 1, both, False, next_blk=blk + 2)
        return 0

    lax.fori_loop(0, i, pair, 0)

    scores(2 * i + 1, 1, bottom)
    consume(2 * i, 0, top, True)
    consume(2 * i, 0, bottom, False)
    consume(2 * i + 1, 1, bottom, True)

    out = acc_ref[:, 0:MLA_V_DIM] * (1.0 / acc_ref[:, MLA_V_DIM:])
    o_ref[...] = (out * _silu(g_ref[...].astype(F32))).astype(o_ref.dtype)


def _mla_attn(q, k, v, main, *, tk):
    s = q.shape[0]
    tq = 2 * tk
    kern = functools.partial(_mla_attn_kernel, tk=tk)
    return pl.pallas_call(
        kern,
        grid=(MLA_HEADS, s // tq),
        in_specs=[
            pl.BlockSpec((tq, MLA_QK_PAD), lambda h, i: (i, h)),
            pl.BlockSpec((s, MLA_QK_PAD), lambda h, i: (0, h)),
            pl.BlockSpec((s, MLA_V_DIM), lambda h, i: (0, h)),
            pl.BlockSpec((tq, MLA_V_DIM), lambda h, i: (i, MLA_G_BLK + h)),
        ],
        out_specs=pl.BlockSpec((tq, MLA_V_DIM), lambda h, i: (i, h)),
        out_shape=jax.ShapeDtypeStruct((s, MLA_WIDTH), BF16),
        scratch_shapes=[pltpu.VMEM((2, tq, tk), F32),
                        pltpu.VMEM((tq, LANES), F32),
                        pltpu.VMEM((tq, MLA_V_DIM + LANES), F32)],
        compiler_params=pltpu.CompilerParams(
            dimension_semantics=("arbitrary", "arbitrary"), vmem_limit_bytes=VMEM_LIMIT_BYTES),
        name="mla_attn",
    )(q, k, v, main)


def _out_proj_kernel(a_ref, b_ref, w_ref, x_ref, nw_ref, o_ref):
    y = (jnp.dot(a_ref[...], w_ref[0:SB_WIDTH, :], preferred_element_type=F32)
         + jnp.dot(b_ref[...], w_ref[SB_WIDTH:, :], preferred_element_type=F32))
    o_ref[...] = x_ref[...] + _rms_norm(y, nw_ref[...])


def _out_proj(mix_a, mix_b, w_out, x, norm_w, *, tm):
    s, d = x.shape
    row = lambda i: (i, 0)
    fixed = lambda i: (0, 0)
    return pl.pallas_call(
        _out_proj_kernel,
        grid=(s // tm,),
        in_specs=[
            pl.BlockSpec((tm, SB_WIDTH), row),
            pl.BlockSpec((tm, MLA_WIDTH), row),
            pl.BlockSpec(w_out.shape, fixed),
            pl.BlockSpec((tm, d), row),
            pl.BlockSpec((1, d), fixed),
        ],
        out_specs=pl.BlockSpec((tm, d), row),
        out_shape=jax.ShapeDtypeStruct((s, d), F32),
        compiler_params=pltpu.CompilerParams(
            dimension_semantics=("arbitrary",), vmem_limit_bytes=VMEM_LIMIT_BYTES),
        name="out_proj",
    )(mix_a, mix_b, w_out, x, norm_w)


def _rotate_half_cols(w):
    w1, w2 = jnp.split(w, 2, axis=-1)
    return jnp.concatenate([-w2, w1], axis=-1)


def _prep_q_up_weight(w_q_up):
    r = w_q_up.shape[0]
    w = w_q_up.reshape(r, MLA_HEADS, MLA_QK_DIM)
    nope, rope = w[:, :, :MLA_NOPE_DIM], w[:, :, MLA_NOPE_DIM:]
    pad = jnp.zeros((r, MLA_HEADS, LANES - MLA_ROPE_DIM), w.dtype)
    wq = jnp.concatenate([nope, rope, pad], axis=2).reshape(r, MLA_HEADS * MLA_QK_PAD)
    wqr = jnp.concatenate([_rotate_half_cols(rope), pad], axis=2).reshape(r, MLA_HEADS * LANES)
    return wq.astype(BF16), wqr.astype(BF16)


def _prep_kv_up_weight(w_kv_up):
    r = w_kv_up.shape[0]
    w = w_kv_up.reshape(r, MLA_HEADS, MLA_NOPE_DIM + MLA_V_DIM)
    k_nope = w[:, :, :MLA_NOPE_DIM].reshape(r, MLA_HEADS * MLA_NOPE_DIM)
    v = w[:, :, MLA_NOPE_DIM:].reshape(r, MLA_WIDTH)
    return jnp.concatenate([k_nope, v], axis=1).astype(BF16)


def _layer(x, cos, sin, pre_norm_w, w_in, q_norm_w, w_q_up, kv_norm_w, w_kv_up, w_out, post_norm_w):
    main, tail = _in_proj(x, pre_norm_w.reshape(1, -1), w_in.T, tm=1024)
    wq, wqr = _prep_q_up_weight(w_q_up)
    q_mla, k_mla, v_mla = _mla_prep(tail, cos, sin, q_norm_w.reshape(1, -1), kv_norm_w.reshape(1, -1),
                                    wq, wqr, _prep_kv_up_weight(w_kv_up), tm=512)
    mix_a = _sb_attn(main, tq=256, tk=256, heads_per_step=4)
    mix_b = _mla_attn(q_mla, k_mla, v_mla, main, tk=512)
    return _out_proj(mix_a, mix_b, w_out.astype(BF16), x, post_norm_w.reshape(1, -1), tm=256)


def kernel(x, positions, pre_norm_w, w_in, q_norm_w, w_q_up, kv_norm_w, w_kv_up, w_out, post_norm_w):
    batch, depth = x.shape[0], pre_norm_w.shape[0]
    outs = []
    for b in range(batch):
        xb = x[b]
        cos, sin = _rope_tables(positions[b])
        for i in range(depth):
            xb = _layer(xb, cos, sin, pre_norm_w[i], w_in[i], q_norm_w[i], w_q_up[i],
                        kv_norm_w[i], w_kv_up[i], w_out[i], post_norm_w[i])
        outs.append(xb)
    return outs[0][None] if batch == 1 else jnp.stack(outs, axis=0)
```

```python
import functools
import math

import numpy as np
import jax
import jax.numpy as jnp
from jax import lax
from jax.experimental import pallas as pl
from jax.experimental.pallas import tpu as pltpu

F32 = jnp.float32
BF16 = jnp.bfloat16

LANES = 128
VMEM_LIMIT_BYTES = 56 * 1024 * 1024

CHUNK = 64
EPS = 1e-6
SB_HEADS = 8
SB_HEAD_DIM = 128
SB_WIDTH = SB_HEADS * SB_HEAD_DIM
MLA_HEADS = 8
MLA_NOPE_DIM = 128
MLA_ROPE_DIM = 64
MLA_QK_DIM = MLA_NOPE_DIM + MLA_ROPE_DIM
MLA_V_DIM = 128
MLA_WIDTH = MLA_HEADS * MLA_V_DIM
Q_LORA_RANK = 512
KV_LORA_RANK = 256
ROPE_THETA = 10000.0
HALF_ROPE = MLA_ROPE_DIM // 2
MLA_QK_PAD = 2 * LANES
LOG2_E = math.log2(math.e)
SB_SKIP_LOG2 = -150.0
MLA_KEY_BLOCK = 512
ONES_ROWS = 16

MAIN_WIDTH = 4 * SB_WIDTH + MLA_WIDTH
TAIL_WIDTH = Q_LORA_RANK + KV_LORA_RANK + 2 * LANES
MLA_G_BLK = 4 * SB_WIDTH // LANES


def _rms_norm(x, w):
    return x * lax.rsqrt(jnp.mean(x * x, axis=-1, keepdims=True) + EPS) * w


def _silu(g):
    return g * (1.0 / (1.0 + jnp.exp(-g)))


def _dot_nt(a, b):
    return lax.dot_general(a, b, (((1,), (1,)), ((), ())), preferred_element_type=F32)


IN_TILE = 512
N_SB = 4 * SB_WIDTH // IN_TILE
N_MAIN = MAIN_WIDTH // IN_TILE
N_IN_TILES = N_MAIN + TAIL_WIDTH // IN_TILE
C_Q_ROW = 4 * SB_WIDTH
C_KV_ROW = C_Q_ROW + Q_LORA_RANK
MLA_GATE_ROW = C_KV_ROW + KV_LORA_RANK + MLA_ROPE_DIM
assert Q_LORA_RANK == IN_TILE and MLA_WIDTH == 2 * IN_TILE


def _in_proj_kernel(x_ref, nw_ref, wt_ref, main_ref, tail_ref, h_ref, *, q_scale):
    j = pl.program_id(1)

    @pl.when(j == 0)
    def _():
        h_ref[...] = _rms_norm(x_ref[...], nw_ref[...]).astype(BF16)

    def proj():
        return _dot_nt(h_ref[...], wt_ref[...].astype(BF16))

    @pl.when(j < SB_WIDTH // IN_TILE)
    def _():
        main_ref[...] = (proj() * q_scale).astype(BF16)

    @pl.when((j >= SB_WIDTH // IN_TILE) & (j < N_MAIN))
    def _():
        main_ref[...] = proj().astype(BF16)

    @pl.when(j >= N_MAIN)
    def _():
        tail_ref[...] = proj()


def _in_proj(x, norm_w, w_in_t, *, tm):
    s, d = x.shape
    kern = functools.partial(_in_proj_kernel, q_scale=LOG2_E / math.sqrt(SB_HEAD_DIM))

    def weight_row(i, j):
        unit = MLA_ROPE_DIM
        step = IN_TILE // unit
        row = jnp.where(j < N_SB, j * step,
                        jnp.where(j < N_MAIN, MLA_GATE_ROW // unit + (j - N_SB) * step,
                                  jnp.where(j == N_MAIN, C_Q_ROW // unit, C_KV_ROW // unit)))
        return (row * unit, 0)

    return pl.pallas_call(
        kern,
        grid=(s // tm, N_IN_TILES),
        in_specs=[
            pl.BlockSpec((tm, d), lambda i, j: (i, 0)),
            pl.BlockSpec((1, d), lambda i, j: (0, 0)),
            pl.BlockSpec((pl.Element(IN_TILE), pl.Element(d)), weight_row),
        ],
        out_specs=[
            pl.BlockSpec((tm, IN_TILE), lambda i, j: (i, jnp.minimum(j, N_MAIN - 1))),
            pl.BlockSpec((tm, IN_TILE), lambda i, j: (i, jnp.maximum(j - N_MAIN, 0))),
        ],
        out_shape=[jax.ShapeDtypeStruct((s, MAIN_WIDTH), BF16),
                   jax.ShapeDtypeStruct((s, TAIL_WIDTH), F32)],
        scratch_shapes=[pltpu.VMEM((tm, d), BF16)],
        compiler_params=pltpu.CompilerParams(
            dimension_semantics=("arbitrary", "arbitrary"), vmem_limit_bytes=VMEM_LIMIT_BYTES),
        name="in_proj",
    )(x, norm_w, w_in_t)


def _rope_table_kernel(pos_ref, freq_ref, cos_ref, sin_ref):
    ang = pos_ref[...].astype(F32) * freq_ref[...]
    cos_ref[...] = jnp.cos(ang)
    sin_ref[...] = jnp.sin(ang)


def _rope_tables(positions):
    s = positions.shape[0]
    per_row = LANES // HALF_ROPE
    inv_freq = ROPE_THETA ** (-jnp.arange(0, MLA_ROPE_DIM, 2, dtype=F32) / MLA_ROPE_DIM)
    pos_rep = jnp.repeat(positions.reshape(s // per_row, per_row), HALF_ROPE, axis=1)
    freq_rep = jnp.tile(inv_freq, per_row).reshape(1, LANES)
    shape = jax.ShapeDtypeStruct((s // per_row, LANES), F32)
    cos, sin = pl.pallas_call(_rope_table_kernel, out_shape=[shape, shape], name="rope_tab")(
        pos_rep, freq_rep)
    zeros = jnp.zeros((s, LANES - MLA_ROPE_DIM), F32)

    def widen(t):
        t = t.reshape(s, HALF_ROPE)
        return jnp.concatenate([t, t, zeros], axis=1)

    return widen(cos), widen(sin)


def _mla_prep_kernel(tail_ref, cos_ref, sin_ref, qnw_ref, kvnw_ref, wq_ref, wqr_ref, wk_ref, wvt_ref,
                     q_ref, k_ref, vt_ref, *, q_scale):
    cos = cos_ref[...]
    sin = sin_ref[...]
    c_q = tail_ref[:, 0:Q_LORA_RANK]
    hq = _rms_norm(c_q, qnw_ref[...]).astype(BF16)
    zq = jnp.dot(hq, wq_ref[...], preferred_element_type=F32)
    zr = jnp.dot(hq, wqr_ref[...], preferred_element_type=F32)
    for h in range(MLA_HEADS):
        lo = h * MLA_QK_PAD
        q_ref[:, lo:lo + LANES] = (zq[:, lo:lo + LANES] * q_scale).astype(BF16)
        rot = zq[:, lo + LANES:lo + 2 * LANES] * cos + zr[:, h * LANES:(h + 1) * LANES] * sin
        q_ref[:, lo + LANES:lo + 2 * LANES] = (rot * q_scale).astype(BF16)

    c_kv = tail_ref[:, Q_LORA_RANK:Q_LORA_RANK + KV_LORA_RANK]
    hkv = _rms_norm(c_kv, kvnw_ref[...]).astype(BF16)
    zk = jnp.dot(hkv, wk_ref[...], preferred_element_type=F32)
    kr0 = Q_LORA_RANK + KV_LORA_RANK
    kr = tail_ref[:, kr0:kr0 + LANES]
    lane = lax.broadcasted_iota(jnp.int32, kr.shape, 1)
    rotated = jnp.where(lane < HALF_ROPE,
                        -pltpu.roll(kr, LANES - HALF_ROPE, axis=1),
                        pltpu.roll(kr, HALF_ROPE, axis=1))
    k_rot = (kr * cos + rotated * sin).astype(BF16)
    for h in range(MLA_HEADS):
        lo = h * MLA_QK_PAD
        k_ref[:, lo:lo + LANES] = zk[:, h * LANES:(h + 1) * LANES].astype(BF16)
        k_ref[:, lo + LANES:lo + 2 * LANES] = k_rot
    vt_ref[0] = _dot_nt(wvt_ref[...], hkv).astype(BF16)


def _mla_prep(tail, cos, sin, q_norm_w, kv_norm_w, wq, wqr, wk, wvt, *, tm):
    s = tail.shape[0]
    row = lambda i: (i, 0)
    fixed = lambda i: (0, 0)
    kern = functools.partial(_mla_prep_kernel, q_scale=LOG2_E / math.sqrt(MLA_QK_DIM))
    return pl.pallas_call(
        kern,
        grid=(s // tm,),
        in_specs=[
            pl.BlockSpec((tm, TAIL_WIDTH), row),
            pl.BlockSpec((tm, LANES), row),
            pl.BlockSpec((tm, LANES), row),
            pl.BlockSpec(q_norm_w.shape, fixed),
            pl.BlockSpec(kv_norm_w.shape, fixed),
            pl.BlockSpec(wq.shape, fixed),
            pl.BlockSpec(wqr.shape, fixed),
            pl.BlockSpec(wk.shape, fixed),
            pl.BlockSpec(wvt.shape, fixed),
        ],
        out_specs=[
            pl.BlockSpec((tm, MLA_HEADS * MLA_QK_PAD), row),
            pl.BlockSpec((tm, MLA_HEADS * MLA_QK_PAD), row),
            pl.BlockSpec((1, MLA_WIDTH, tm), lambda i: (i, 0, 0)),
        ],
        out_shape=[jax.ShapeDtypeStruct((s, MLA_HEADS * MLA_QK_PAD), BF16),
                   jax.ShapeDtypeStruct((s, MLA_HEADS * MLA_QK_PAD), BF16),
                   jax.ShapeDtypeStruct((s // tm, MLA_WIDTH, tm), BF16)],
        compiler_params=pltpu.CompilerParams(
            dimension_semantics=("arbitrary",), vmem_limit_bytes=VMEM_LIMIT_BYTES),
        name="mla_prep",
    )(tail, cos, sin, q_norm_w, kv_norm_w, wq, wqr, wk, wvt)


def _sb_attn_kernel(q_ref, k_ref, v_ref, g_ref, o_ref, tri_ref, acc_ref, carry_ref, *, tq, tk):
    i = pl.program_id(1)
    n_heads = q_ref.shape[1] // SB_HEAD_DIM

    rows = lax.broadcasted_iota(jnp.int32, (tk, tk), 0)
    cols = lax.broadcasted_iota(jnp.int32, (tk, tk), 1)
    tri = jnp.where(rows >= cols, -1.0, 0.0).astype(BF16)
    tri_ref[0:tk, :] = tri
    tri_ref[tk:2 * tk, :] = tri
    acc_ref[...] = jnp.zeros_like(acc_ref)
    carry_ref[...] = jnp.zeros_like(carry_ref)

    def block(start, masked):
        heads = range(n_heads)
        lanes = [slice(h * SB_HEAD_DIM, (h + 1) * SB_HEAD_DIM) for h in heads]
        z = [_dot_nt(q_ref[:, lanes[h]], k_ref[pl.ds(start, tk), lanes[h]]) for h in heads]
        sp = [jnp.maximum(z[h], 0.0) + jnp.log2(1.0 + jnp.exp2(-jnp.abs(z[h]))) for h in heads]
        if masked:
            t_idx = i * tq + lax.broadcasted_iota(jnp.int32, (tq, tk), 0)
            s_idx = start + lax.broadcasted_iota(jnp.int32, (tq, tk), 1)
            before = s_idx < t_idx
            sp = [jnp.where(before, sp[h], 0.0) for h in heads]
        tri = tri_ref[...]
        sums = []
        for h in heads:
            hi = sp[h].astype(BF16)
            lo = (sp[h] - hi.astype(F32)).astype(BF16)
            sums.append(jnp.dot(jnp.concatenate([hi, lo], axis=1), tri,
                                preferred_element_type=F32))
        carry = [carry_ref[:, lanes[h]] for h in heads]
        a = [jnp.exp2(z[h] + sums[h] + jnp.concatenate([carry[h]] * (tk // LANES), axis=1))
             for h in heads]
        if masked:
            a = [jnp.where(before, a[h], 0.0) for h in heads]
        for h in heads:
            acc_ref[:, lanes[h]] += jnp.dot(a[h].astype(BF16), v_ref[pl.ds(start, tk), lanes[h]],
                                            preferred_element_type=F32)
            carry_ref[:, lanes[h]] = carry[h] + jnp.broadcast_to(sums[h][:, 0:1], (tq, LANES))

    n_diag = tq // tk
    for d in reversed(range(n_diag)):
        block(pl.multiple_of(i * tq + d * tk, tk), True)

    def live():
        return (jnp.max(carry_ref[...]) > SB_SKIP_LOG2).astype(jnp.int32)

    def cond(state):
        j, alive = state
        return (j >= 0) & (alive > 0)

    def body(state):
        j, _ = state
        block(pl.multiple_of(j * tk, tk), False)
        return j - 1, live()

    lax.while_loop(cond, body, (i * n_diag - 1, live()))
    o_ref[...] = (acc_ref[...] * _silu(g_ref[...].astype(F32))).astype(o_ref.dtype)


def _sb_attn(main, *, tq, tk, heads_per_step):
    s = main.shape[0]
    n_groups = SB_HEADS // heads_per_step
    width = heads_per_step * SB_HEAD_DIM
    kern = functools.partial(_sb_attn_kernel, tq=tq, tk=tk)
    return pl.pallas_call(
        kern,
        grid=(n_groups, s // tq),
        in_specs=[
            pl.BlockSpec((tq, width), lambda g, i: (i, g)),
            pl.BlockSpec((s, width), lambda g, i: (0, n_groups + g)),
            pl.BlockSpec((s, width), lambda g, i: (0, 2 * n_groups + g)),
            pl.BlockSpec((tq, width), lambda g, i: (i, 3 * n_groups + g)),
        ],
        out_specs=pl.BlockSpec((tq, width), lambda g, i: (i, g)),
        out_shape=jax.ShapeDtypeStruct((s, SB_WIDTH), BF16),
        scratch_shapes=[pltpu.VMEM((2 * tk, tk), BF16),
                        pltpu.VMEM((tq, width), F32),
                        pltpu.VMEM((tq, width), F32)],
        compiler_params=pltpu.CompilerParams(
            dimension_semantics=("arbitrary", "arbitrary"), vmem_limit_bytes=VMEM_LIMIT_BYTES),
        name="sb_attn",
    )(main, main, main, main)


def _mla_attn_kernel(q_ref, k_ref, vt_ref, g_ref, o_ref, s_ref, m_ref, acc_ref, *, tk):
    i = pl.program_id(1)
    tq = 2 * tk
    col_tile = 256
    left, right, both = slice(0, tk), slice(tk, tq), slice(0, tq)
    ones = jnp.ones((ONES_ROWS, tk), BF16)
    m_ref[...] = jnp.full_like(m_ref, -jnp.inf)
    acc_ref[...] = jnp.zeros_like(acc_ref)

    def scores(blk, slot, cols):
        start = pl.multiple_of(blk * tk, tk)
        s_ref[slot, :, cols] = _dot_nt(k_ref[pl.ds(start, tk), :], q_ref[cols, :])

    def tiles(blk, slot, cols, masked, next_blk=None):
        return [(blk, slot, slice(c0, c0 + col_tile), c0 - cols.start, masked, next_blk)
                for c0 in range(cols.start, cols.stop, col_tile)]

    def softmax_stage(item):
        blk, slot, cols, col_offset, masked, _ = item
        s = s_ref[slot, :, cols]
        if masked:
            s_chunk = lax.broadcasted_iota(jnp.int32, (tk, col_tile), 0) // CHUNK
            t_chunk = (col_offset + lax.broadcasted_iota(jnp.int32, (tk, col_tile), 1)) // CHUNK
            s = jnp.where(s_chunk <= t_chunk, s, -jnp.inf)
        m_prev = m_ref[:, cols]
        m_new = jnp.maximum(m_prev, jnp.max(s, axis=0, keepdims=True))
        m_ref[:, cols] = m_new
        return blk, cols, jnp.exp2(m_prev - m_new), jnp.exp2((s - m_new).astype(BF16))

    def value_stage(blk, cols, alpha, p):
        vt_ones = jnp.concatenate([vt_ref[blk], ones], axis=0)
        acc_ref[:, cols] = alpha * acc_ref[:, cols] + jnp.dot(vt_ones, p, preferred_element_type=F32)

    def fold(items):
        pending = None
        for item in items:
            _, slot, cols, _, _, next_blk = item
            if next_blk is not None:
                scores(next_blk, 1 - slot, cols)
            done = softmax_stage(item)
            if pending is not None:
                value_stage(*pending)
            pending = done
        value_stage(*pending)

    scores(0, 0, both)

    def pair(n, _):
        blk = 2 * n
        fold(tiles(blk, 0, both, False, next_blk=blk + 1)
             + tiles(blk + 1, 1, both, False, next_blk=blk + 2))
        return 0

    lax.fori_loop(0, i, pair, 0)

    scores(2 * i + 1, 1, right)
    fold(tiles(2 * i, 0, left, True) + tiles(2 * i, 0, right, False)
         + tiles(2 * i + 1, 1, right, True))

    out_t = acc_ref[0:MLA_V_DIM, :] * (1.0 / acc_ref[MLA_V_DIM:MLA_V_DIM + 1, :])
    o_ref[...] = (out_t.T * _silu(g_ref[...].astype(F32))).astype(o_ref.dtype)


def _mla_attn(q, k, vt, main, *, tk):
    s = q.shape[0]
    tq = 2 * tk
    kern = functools.partial(_mla_attn_kernel, tk=tk)
    return pl.pallas_call(
        kern,
        grid=(MLA_HEADS, s // tq),
        in_specs=[
            pl.BlockSpec((tq, MLA_QK_PAD), lambda h, i: (i, h)),
            pl.BlockSpec((s, MLA_QK_PAD), lambda h, i: (0, h)),
            pl.BlockSpec((s // tk, MLA_V_DIM, tk), lambda h, i: (0, h, 0)),
            pl.BlockSpec((tq, MLA_V_DIM), lambda h, i: (i, MLA_G_BLK + h)),
        ],
        out_specs=pl.BlockSpec((tq, MLA_V_DIM), lambda h, i: (i, h)),
        out_shape=jax.ShapeDtypeStruct((s, MLA_WIDTH), BF16),
        scratch_shapes=[pltpu.VMEM((2, tk, tq), F32),
                        pltpu.VMEM((1, tq), F32),
                        pltpu.VMEM((MLA_V_DIM + ONES_ROWS, tq), F32)],
        compiler_params=pltpu.CompilerParams(
            dimension_semantics=("arbitrary", "arbitrary"), vmem_limit_bytes=VMEM_LIMIT_BYTES),
        name="mla_attn",
    )(q, k, vt, main)


def _out_proj_kernel(a_ref, b_ref, w_ref, x_ref, nw_ref, o_ref):
    y = (jnp.dot(a_ref[...], w_ref[0:SB_WIDTH, :], preferred_element_type=F32)
         + jnp.dot(b_ref[...], w_ref[SB_WIDTH:, :], preferred_element_type=F32))
    o_ref[...] = x_ref[...] + _rms_norm(y, nw_ref[...])


def _out_proj(mix_a, mix_b, w_out, x, norm_w, *, tm):
    s, d = x.shape
    row = lambda i: (i, 0)
    fixed = lambda i: (0, 0)
    return pl.pallas_call(
        _out_proj_kernel,
        grid=(s // tm,),
        in_specs=[
            pl.BlockSpec((tm, SB_WIDTH), row),
            pl.BlockSpec((tm, MLA_WIDTH), row),
            pl.BlockSpec(w_out.shape, fixed),
            pl.BlockSpec((tm, d), row),
            pl.BlockSpec((1, d), fixed),
        ],
        out_specs=pl.BlockSpec((tm, d), row),
        out_shape=jax.ShapeDtypeStruct((s, d), F32),
        compiler_params=pltpu.CompilerParams(
            dimension_semantics=("arbitrary",), vmem_limit_bytes=VMEM_LIMIT_BYTES),
        name="out_proj",
    )(mix_a, mix_b, w_out, x, norm_w)


def _rotate_half_cols(w):
    w1, w2 = jnp.split(w, 2, axis=-1)
    return jnp.concatenate([-w2, w1], axis=-1)


def _prep_q_up_weight(w_q_up):
    r = w_q_up.shape[0]
    w = w_q_up.reshape(r, MLA_HEADS, MLA_QK_DIM)
    nope, rope = w[:, :, :MLA_NOPE_DIM], w[:, :, MLA_NOPE_DIM:]
    pad = jnp.zeros((r, MLA_HEADS, LANES - MLA_ROPE_DIM), w.dtype)
    wq = jnp.concatenate([nope, rope, pad], axis=2).reshape(r, MLA_HEADS * MLA_QK_PAD)
    wqr = jnp.concatenate([_rotate_half_cols(rope), pad], axis=2).reshape(r, MLA_HEADS * LANES)
    return wq.astype(BF16), wqr.astype(BF16)


def _prep_kv_up_weight(w_kv_up):
    r = w_kv_up.shape[0]
    w = w_kv_up.reshape(r, MLA_HEADS, MLA_NOPE_DIM + MLA_V_DIM)
    k_nope = w[:, :, :MLA_NOPE_DIM].reshape(r, MLA_HEADS * MLA_NOPE_DIM)
    v = w[:, :, MLA_NOPE_DIM:].reshape(r, MLA_WIDTH)
    return k_nope.astype(BF16), v.T.astype(BF16)


def _layer(x, cos, sin, pre_norm_w, w_in, q_norm_w, w_q_up, kv_norm_w, w_kv_up, w_out, post_norm_w):
    main, tail = _in_proj(x, pre_norm_w.reshape(1, -1), w_in.T, tm=1024)
    wq, wqr = _prep_q_up_weight(w_q_up)
    wk, wvt = _prep_kv_up_weight(w_kv_up)
    q_mla, k_mla, vt_mla = _mla_prep(tail, cos, sin, q_norm_w.reshape(1, -1),
                                     kv_norm_w.reshape(1, -1), wq, wqr, wk, wvt, tm=MLA_KEY_BLOCK)
    mix_a = _sb_attn(main, tq=256, tk=256, heads_per_step=4)
    mix_b = _mla_attn(q_mla, k_mla, vt_mla, main, tk=MLA_KEY_BLOCK)
    return _out_proj(mix_a, mix_b, w_out.astype(BF16), x, post_norm_w.reshape(1, -1), tm=256)


def kernel(x, positions, pre_norm_w, w_in, q_norm_w, w_q_up, kv_norm_w, w_kv_up, w_out, post_norm_w):
    batch, depth = x.shape[0], pre_norm_w.shape[0]
    outs = []
    for b in range(batch):
        xb = x[b]
        cos, sin = _rope_tables(positions[b])
        for i in range(depth):
            xb = _layer(xb, cos, sin, pre_norm_w[i], w_in[i], q_norm_w[i], w_q_up[i],
                        kv_norm_w[i], w_kv_up[i], w_out[i], post_norm_w[i])
        outs.append(xb)
    return outs[0][None] if batch == 1 else jnp.stack(outs, axis=0)
```

```python
import functools
import math

import numpy as np
import jax
import jax.numpy as jnp
from jax import lax
from jax.experimental import pallas as pl
from jax.experimental.pallas import tpu as pltpu

F32 = jnp.float32
BF16 = jnp.bfloat16

LANES = 128
VMEM_LIMIT_BYTES = 56 * 1024 * 1024

CHUNK = 64
EPS = 1e-6
SB_HEADS = 8
SB_HEAD_DIM = 128
SB_WIDTH = SB_HEADS * SB_HEAD_DIM
MLA_HEADS = 8
MLA_NOPE_DIM = 128
MLA_ROPE_DIM = 64
MLA_QK_DIM = MLA_NOPE_DIM + MLA_ROPE_DIM
MLA_V_DIM = 128
MLA_WIDTH = MLA_HEADS * MLA_V_DIM
Q_LORA_RANK = 512
KV_LORA_RANK = 256
ROPE_THETA = 10000.0
HALF_ROPE = MLA_ROPE_DIM // 2
MLA_QK_PAD = 2 * LANES
LOG2_E = math.log2(math.e)
SB_SKIP_LOG2 = -150.0

MAIN_WIDTH = 4 * SB_WIDTH + MLA_WIDTH
TAIL_WIDTH = Q_LORA_RANK + KV_LORA_RANK + 2 * LANES
MLA_G_BLK = 4 * SB_WIDTH // LANES


def _rms_norm(x, w):
    return x * lax.rsqrt(jnp.mean(x * x, axis=-1, keepdims=True) + EPS) * w


def _silu(g):
    return g * (1.0 / (1.0 + jnp.exp(-g)))


def _dot_nt(a, b):
    return lax.dot_general(a, b, (((1,), (1,)), ((), ())), preferred_element_type=F32)


IN_TILE = 512
N_SB = 4 * SB_WIDTH // IN_TILE
N_MAIN = MAIN_WIDTH // IN_TILE
N_IN_TILES = N_MAIN + TAIL_WIDTH // IN_TILE
C_Q_ROW = 4 * SB_WIDTH
C_KV_ROW = C_Q_ROW + Q_LORA_RANK
MLA_GATE_ROW = C_KV_ROW + KV_LORA_RANK + MLA_ROPE_DIM
assert Q_LORA_RANK == IN_TILE and MLA_WIDTH == 2 * IN_TILE


def _in_proj_kernel(x_ref, nw_ref, wt_ref, main_ref, tail_ref, h_ref, *, q_scale):
    j = pl.program_id(1)

    @pl.when(j == 0)
    def _():
        h_ref[...] = _rms_norm(x_ref[...], nw_ref[...]).astype(BF16)

    def proj():
        return _dot_nt(h_ref[...], wt_ref[...].astype(BF16))

    @pl.when(j < SB_WIDTH // IN_TILE)
    def _():
        main_ref[...] = (proj() * q_scale).astype(BF16)

    @pl.when((j >= SB_WIDTH // IN_TILE) & (j < N_MAIN))
    def _():
        main_ref[...] = proj().astype(BF16)

    @pl.when(j >= N_MAIN)
    def _():
        tail_ref[...] = proj()


def _in_proj(x, norm_w, w_in_t, *, tm):
    s, d = x.shape
    kern = functools.partial(_in_proj_kernel, q_scale=LOG2_E / math.sqrt(SB_HEAD_DIM))

    def weight_row(i, j):
        unit = MLA_ROPE_DIM
        step = IN_TILE // unit
        row = jnp.where(j < N_SB, j * step,
                        jnp.where(j < N_MAIN, MLA_GATE_ROW // unit + (j - N_SB) * step,
                                  jnp.where(j == N_MAIN, C_Q_ROW // unit, C_KV_ROW // unit)))
        return (row * unit, 0)

    return pl.pallas_call(
        kern,
        grid=(s // tm, N_IN_TILES),
        in_specs=[
            pl.BlockSpec((tm, d), lambda i, j: (i, 0)),
            pl.BlockSpec((1, d), lambda i, j: (0, 0)),
            pl.BlockSpec((pl.Element(IN_TILE), pl.Element(d)), weight_row),
        ],
        out_specs=[
            pl.BlockSpec((tm, IN_TILE), lambda i, j: (i, jnp.minimum(j, N_MAIN - 1))),
            pl.BlockSpec((tm, IN_TILE), lambda i, j: (i, jnp.maximum(j - N_MAIN, 0))),
        ],
        out_shape=[jax.ShapeDtypeStruct((s, MAIN_WIDTH), BF16),
                   jax.ShapeDtypeStruct((s, TAIL_WIDTH), F32)],
        scratch_shapes=[pltpu.VMEM((tm, d), BF16)],
        compiler_params=pltpu.CompilerParams(
            dimension_semantics=("arbitrary", "arbitrary"), vmem_limit_bytes=VMEM_LIMIT_BYTES),
        name="in_proj",
    )(x, norm_w, w_in_t)


def _rope_table_kernel(pos_ref, freq_ref, cos_ref, sin_ref):
    ang = pos_ref[...].astype(F32) * freq_ref[...]
    cos_ref[...] = jnp.cos(ang)
    sin_ref[...] = jnp.sin(ang)


def _rope_tables(positions):
    s = positions.shape[0]
    per_row = LANES // HALF_ROPE
    inv_freq = ROPE_THETA ** (-jnp.arange(0, MLA_ROPE_DIM, 2, dtype=F32) / MLA_ROPE_DIM)
    pos_rep = jnp.repeat(positions.reshape(s // per_row, per_row), HALF_ROPE, axis=1)
    freq_rep = jnp.tile(inv_freq, per_row).reshape(1, LANES)
    shape = jax.ShapeDtypeStruct((s // per_row, LANES), F32)
    cos, sin = pl.pallas_call(_rope_table_kernel, out_shape=[shape, shape], name="rope_tab")(
        pos_rep, freq_rep)
    zeros = jnp.zeros((s, LANES - MLA_ROPE_DIM), F32)

    def widen(t):
        t = t.reshape(s, HALF_ROPE)
        return jnp.concatenate([t, t, zeros], axis=1)

    return widen(cos), widen(sin)


def _mla_prep_kernel(tail_ref, cos_ref, sin_ref, qnw_ref, kvnw_ref, wq_ref, wqr_ref, wkv_ref,
                     q_ref, k_ref, v_ref, *, q_scale):
    cos = cos_ref[...]
    sin = sin_ref[...]
    c_q = tail_ref[:, 0:Q_LORA_RANK]
    hq = _rms_norm(c_q, qnw_ref[...]).astype(BF16)
    zq = jnp.dot(hq, wq_ref[...], preferred_element_type=F32)
    zr = jnp.dot(hq, wqr_ref[...], preferred_element_type=F32)
    for h in range(MLA_HEADS):
        lo = h * MLA_QK_PAD
        q_ref[:, lo:lo + LANES] = (zq[:, lo:lo + LANES] * q_scale).astype(BF16)
        rot = zq[:, lo + LANES:lo + 2 * LANES] * cos + zr[:, h * LANES:(h + 1) * LANES] * sin
        q_ref[:, lo + LANES:lo + 2 * LANES] = (rot * q_scale).astype(BF16)

    c_kv = tail_ref[:, Q_LORA_RANK:Q_LORA_RANK + KV_LORA_RANK]
    hkv = _rms_norm(c_kv, kvnw_ref[...]).astype(BF16)
    zkv = jnp.dot(hkv, wkv_ref[...], preferred_element_type=F32)
    kr0 = Q_LORA_RANK + KV_LORA_RANK
    kr = tail_ref[:, kr0:kr0 + LANES]
    lane = lax.broadcasted_iota(jnp.int32, kr.shape, 1)
    rotated = jnp.where(lane < HALF_ROPE,
                        -pltpu.roll(kr, LANES - HALF_ROPE, axis=1),
                        pltpu.roll(kr, HALF_ROPE, axis=1))
    k_rot = (kr * cos + rotated * sin).astype(BF16)
    for h in range(MLA_HEADS):
        lo = h * MLA_QK_PAD
        k_ref[:, lo:lo + LANES] = zkv[:, h * LANES:(h + 1) * LANES].astype(BF16)
        k_ref[:, lo + LANES:lo + 2 * LANES] = k_rot
    v_ref[...] = zkv[:, MLA_HEADS * MLA_NOPE_DIM:].astype(BF16)


def _mla_prep(tail, cos, sin, q_norm_w, kv_norm_w, wq, wqr, wkv, *, tm):
    s = tail.shape[0]
    row = lambda i: (i, 0)
    fixed = lambda i: (0, 0)
    kern = functools.partial(_mla_prep_kernel, q_scale=LOG2_E / math.sqrt(MLA_QK_DIM))
    return pl.pallas_call(
        kern,
        grid=(s // tm,),
        in_specs=[
            pl.BlockSpec((tm, TAIL_WIDTH), row),
            pl.BlockSpec((tm, LANES), row),
            pl.BlockSpec((tm, LANES), row),
            pl.BlockSpec(q_norm_w.shape, fixed),
            pl.BlockSpec(kv_norm_w.shape, fixed),
            pl.BlockSpec(wq.shape, fixed),
            pl.BlockSpec(wqr.shape, fixed),
            pl.BlockSpec(wkv.shape, fixed),
        ],
        out_specs=[
            pl.BlockSpec((tm, MLA_HEADS * MLA_QK_PAD), row),
            pl.BlockSpec((tm, MLA_HEADS * MLA_QK_PAD), row),
            pl.BlockSpec((tm, MLA_WIDTH), row),
        ],
        out_shape=[jax.ShapeDtypeStruct((s, MLA_HEADS * MLA_QK_PAD), BF16),
                   jax.ShapeDtypeStruct((s, MLA_HEADS * MLA_QK_PAD), BF16),
                   jax.ShapeDtypeStruct((s, MLA_WIDTH), BF16)],
        compiler_params=pltpu.CompilerParams(
            dimension_semantics=("arbitrary",), vmem_limit_bytes=VMEM_LIMIT_BYTES),
        name="mla_prep",
    )(tail, cos, sin, q_norm_w, kv_norm_w, wq, wqr, wkv)


def _sb_attn_kernel(q_ref, k_ref, v_ref, g_ref, o_ref, tri_ref, acc_ref, carry_ref, *, tq, tk):
    i = pl.program_id(1)
    n_heads = q_ref.shape[1] // SB_HEAD_DIM

    rows = lax.broadcasted_iota(jnp.int32, (tk, tk), 0)
    cols = lax.broadcasted_iota(jnp.int32, (tk, tk), 1)
    tri = jnp.where(rows >= cols, -1.0, 0.0).astype(BF16)
    tri_ref[0:tk, :] = tri
    tri_ref[tk:2 * tk, :] = tri
    acc_ref[...] = jnp.zeros_like(acc_ref)
    carry_ref[...] = jnp.zeros_like(carry_ref)

    def block(start, masked):
        heads = range(n_heads)
        lanes = [slice(h * SB_HEAD_DIM, (h + 1) * SB_HEAD_DIM) for h in heads]
        z = [_dot_nt(q_ref[:, lanes[h]], k_ref[pl.ds(start, tk), lanes[h]]) for h in heads]
        sp = [jnp.maximum(z[h], 0.0) + jnp.log2(1.0 + jnp.exp2(-jnp.abs(z[h]))) for h in heads]
        if masked:
            t_idx = i * tq + lax.broadcasted_iota(jnp.int32, (tq, tk), 0)
            s_idx = start + lax.broadcasted_iota(jnp.int32, (tq, tk), 1)
            before = s_idx < t_idx
            sp = [jnp.where(before, sp[h], 0.0) for h in heads]
        tri = tri_ref[...]
        sums = []
        for h in heads:
            hi = sp[h].astype(BF16)
            lo = (sp[h] - hi.astype(F32)).astype(BF16)
            sums.append(jnp.dot(jnp.concatenate([hi, lo], axis=1), tri,
                                preferred_element_type=F32))
        carry = [carry_ref[:, lanes[h]] for h in heads]
        a = [jnp.exp2(z[h] + sums[h] + jnp.concatenate([carry[h]] * (tk // LANES), axis=1))
             for h in heads]
        if masked:
            a = [jnp.where(before, a[h], 0.0) for h in heads]
        for h in heads:
            acc_ref[:, lanes[h]] += jnp.dot(a[h].astype(BF16), v_ref[pl.ds(start, tk), lanes[h]],
                                            preferred_element_type=F32)
            carry_ref[:, lanes[h]] = carry[h] + jnp.broadcast_to(sums[h][:, 0:1], (tq, LANES))

    n_diag = tq // tk
    for d in reversed(range(n_diag)):
        block(pl.multiple_of(i * tq + d * tk, tk), True)

    def live():
        return (jnp.max(carry_ref[...]) > SB_SKIP_LOG2).astype(jnp.int32)

    def cond(state):
        j, alive = state
        return (j >= 0) & (alive > 0)

    def body(state):
        j, _ = state
        block(pl.multiple_of(j * tk, tk), False)
        return j - 1, live()

    lax.while_loop(cond, body, (i * n_diag - 1, live()))
    o_ref[...] = (acc_ref[...] * _silu(g_ref[...].astype(F32))).astype(o_ref.dtype)


def _sb_attn(main, *, tq, tk, heads_per_step):
    s = main.shape[0]
    n_groups = SB_HEADS // heads_per_step
    width = heads_per_step * SB_HEAD_DIM
    kern = functools.partial(_sb_attn_kernel, tq=tq, tk=tk)
    return pl.pallas_call(
        kern,
        grid=(n_groups, s // tq),
        in_specs=[
            pl.BlockSpec((tq, width), lambda g, i: (i, g)),
            pl.BlockSpec((s, width), lambda g, i: (0, n_groups + g)),
            pl.BlockSpec((s, width), lambda g, i: (0, 2 * n_groups + g)),
            pl.BlockSpec((tq, width), lambda g, i: (i, 3 * n_groups + g)),
        ],
        out_specs=pl.BlockSpec((tq, width), lambda g, i: (i, g)),
        out_shape=jax.ShapeDtypeStruct((s, SB_WIDTH), BF16),
        scratch_shapes=[pltpu.VMEM((2 * tk, tk), BF16),
                        pltpu.VMEM((tq, width), F32),
                        pltpu.VMEM((tq, width), F32)],
        compiler_params=pltpu.CompilerParams(
            dimension_semantics=("arbitrary", "arbitrary"), vmem_limit_bytes=VMEM_LIMIT_BYTES),
        name="sb_attn",
    )(main, main, main, main)


def _mla_attn_kernel(q_ref, k_ref, v_ref, g_ref, o_ref, s_ref, m_ref, acc_ref, *, tk):
    i = pl.program_id(1)
    tq = 2 * tk
    n_col = tk // LANES
    n_heads = v_ref.shape[1] // MLA_V_DIM
    heads = range(n_heads)
    row_tile = 256
    top, bottom, both = slice(0, tk), slice(tk, tq), slice(0, tq)
    ones = jnp.ones((tk, LANES), BF16)
    m_ref[...] = jnp.full_like(m_ref, -jnp.inf)
    acc_ref[...] = jnp.zeros_like(acc_ref)

    def qk_lanes(head):
        return slice(head * MLA_QK_PAD, (head + 1) * MLA_QK_PAD)

    def v_lanes(head):
        return slice(head * MLA_V_DIM, (head + 1) * MLA_V_DIM)

    def scores(head, blk, slot, rows):
        start = pl.multiple_of(blk * tk, tk)
        s_ref[head, slot, rows] = _dot_nt(q_ref[rows, qk_lanes(head)],
                                          k_ref[pl.ds(start, tk), qk_lanes(head)])

    def tiles(blk, slot, rows, masked, next_blk=None):
        return [(head, blk, slot, slice(r0, r0 + row_tile), r0 - rows.start, masked, next_blk)
                for r0 in range(rows.start, rows.stop, row_tile) for head in heads]

    def softmax_stage(item):
        head, blk, slot, rows, row_offset, masked, _ = item

        def column(c):
            s = s_ref[head, slot, rows, c * LANES:(c + 1) * LANES]
            if masked:
                t_chunk = (row_offset + lax.broadcasted_iota(jnp.int32, (row_tile, LANES), 0)) // CHUNK
                s_chunk = (c * LANES + lax.broadcasted_iota(jnp.int32, (row_tile, LANES), 1)) // CHUNK
                s = jnp.where(s_chunk <= t_chunk, s, -jnp.inf)
            return s

        col_max = column(0)
        for c in range(1, n_col):
            col_max = jnp.maximum(col_max, column(c))
        m_prev = m_ref[head, rows]
        m_new = jnp.maximum(m_prev, jnp.max(col_max, axis=-1, keepdims=True))
        m_ref[head, rows] = m_new
        p = jnp.concatenate([jnp.exp2((column(c) - m_new).astype(BF16)) for c in range(n_col)],
                            axis=1)
        return head, blk, rows, jnp.exp2(m_prev - m_new), p

    def value_stage(head, blk, rows, alpha, p):
        start = pl.multiple_of(blk * tk, tk)
        v_ones = jnp.concatenate([v_ref[pl.ds(start, tk), v_lanes(head)], ones], axis=1)
        acc_ref[head, rows] = (jnp.concatenate([alpha, alpha], axis=1) * acc_ref[head, rows]
                               + jnp.dot(p, v_ones, preferred_element_type=F32))

    def fold(items):
        for item in items:
            head, _, slot, rows, _, _, next_blk = item
            if next_blk is not None:
                scores(head, next_blk, 1 - slot, rows)
            value_stage(*softmax_stage(item))

    for head in heads:
        scores(head, 0, 0, both)

    def pair(n, _):
        blk = 2 * n
        fold(tiles(blk, 0, both, False, next_blk=blk + 1)
             + tiles(blk + 1, 1, both, False, next_blk=blk + 2))
        return 0

    lax.fori_loop(0, i, pair, 0)

    for head in heads:
        scores(head, 2 * i + 1, 1, bottom)
    fold(tiles(2 * i, 0, top, True) + tiles(2 * i, 0, bottom, False)
         + tiles(2 * i + 1, 1, bottom, True))

    for head in heads:
        out = acc_ref[head, :, 0:MLA_V_DIM] * (1.0 / acc_ref[head, :, MLA_V_DIM:])
        gate = g_ref[:, v_lanes(head)].astype(F32)
        o_ref[:, v_lanes(head)] = (out * _silu(gate)).astype(o_ref.dtype)


def _mla_attn(q, k, v, main, *, tk, heads_per_step):
    s = q.shape[0]
    tq = 2 * tk
    n_groups = MLA_HEADS // heads_per_step
    qk_width = heads_per_step * MLA_QK_PAD
    v_width = heads_per_step * MLA_V_DIM
    gate_block = MLA_G_BLK // heads_per_step
    kern = functools.partial(_mla_attn_kernel, tk=tk)
    return pl.pallas_call(
        kern,
        grid=(n_groups, s // tq),
        in_specs=[
            pl.BlockSpec((tq, qk_width), lambda g, i: (i, g)),
            pl.BlockSpec((s, qk_width), lambda g, i: (0, g)),
            pl.BlockSpec((s, v_width), lambda g, i: (0, g)),
            pl.BlockSpec((tq, v_width), lambda g, i: (i, gate_block + g)),
        ],
        out_specs=pl.BlockSpec((tq, v_width), lambda g, i: (i, g)),
        out_shape=jax.ShapeDtypeStruct((s, MLA_WIDTH), BF16),
        scratch_shapes=[pltpu.VMEM((heads_per_step, 2, tq, tk), F32),
                        pltpu.VMEM((heads_per_step, tq, LANES), F32),
                        pltpu.VMEM((heads_per_step, tq, MLA_V_DIM + LANES), F32)],
        compiler_params=pltpu.CompilerParams(
            dimension_semantics=("arbitrary", "arbitrary"), vmem_limit_bytes=VMEM_LIMIT_BYTES),
        name="mla_attn",
    )(q, k, v, main)


def _out_proj_kernel(a_ref, b_ref, w_ref, x_ref, nw_ref, o_ref):
    y = (jnp.dot(a_ref[...], w_ref[0:SB_WIDTH, :], preferred_element_type=F32)
         + jnp.dot(b_ref[...], w_ref[SB_WIDTH:, :], preferred_element_type=F32))
    o_ref[...] = x_ref[...] + _rms_norm(y, nw_ref[...])


def _out_proj(mix_a, mix_b, w_out, x, norm_w, *, tm):
    s, d = x.shape
    row = lambda i: (i, 0)
    fixed = lambda i: (0, 0)
    return pl.pallas_call(
        _out_proj_kernel,
        grid=(s // tm,),
        in_specs=[
            pl.BlockSpec((tm, SB_WIDTH), row),
            pl.BlockSpec((tm, MLA_WIDTH), row),
            pl.BlockSpec(w_out.shape, fixed),
            pl.BlockSpec((tm, d), row),
            pl.BlockSpec((1, d), fixed),
        ],
        out_specs=pl.BlockSpec((tm, d), row),
        out_shape=jax.ShapeDtypeStruct((s, d), F32),
        compiler_params=pltpu.CompilerParams(
            dimension_semantics=("arbitrary",), vmem_limit_bytes=VMEM_LIMIT_BYTES),
        name="out_proj",
    )(mix_a, mix_b, w_out, x, norm_w)


def _rotate_half_cols(w):
    w1, w2 = jnp.split(w, 2, axis=-1)
    return jnp.concatenate([-w2, w1], axis=-1)


def _prep_q_up_weight(w_q_up):
    r = w_q_up.shape[0]
    w = w_q_up.reshape(r, MLA_HEADS, MLA_QK_DIM)
    nope, rope = w[:, :, :MLA_NOPE_DIM], w[:, :, MLA_NOPE_DIM:]
    pad = jnp.zeros((r, MLA_HEADS, LANES - MLA_ROPE_DIM), w.dtype)
    wq = jnp.concatenate([nope, rope, pad], axis=2).reshape(r, MLA_HEADS * MLA_QK_PAD)
    wqr = jnp.concatenate([_rotate_half_cols(rope), pad], axis=2).reshape(r, MLA_HEADS * LANES)
    return wq.astype(BF16), wqr.astype(BF16)


def _prep_kv_up_weight(w_kv_up):
    r = w_kv_up.shape[0]
    w = w_kv_up.reshape(r, MLA_HEADS, MLA_NOPE_DIM + MLA_V_DIM)
    k_nope = w[:, :, :MLA_NOPE_DIM].reshape(r, MLA_HEADS * MLA_NOPE_DIM)
    v = w[:, :, MLA_NOPE_DIM:].reshape(r, MLA_WIDTH)
    return jnp.concatenate([k_nope, v], axis=1).astype(BF16)


def _layer(x, cos, sin, pre_norm_w, w_in, q_norm_w, w_q_up, kv_norm_w, w_kv_up, w_out, post_norm_w):
    main, tail = _in_proj(x, pre_norm_w.reshape(1, -1), w_in.T, tm=1024)
    wq, wqr = _prep_q_up_weight(w_q_up)
    q_mla, k_mla, v_mla = _mla_prep(tail, cos, sin, q_norm_w.reshape(1, -1), kv_norm_w.reshape(1, -1),
                                    wq, wqr, _prep_kv_up_weight(w_kv_up), tm=512)
    mix_a = _sb_attn(main, tq=256, tk=256, heads_per_step=4)
    mix_b = _mla_attn(q_mla, k_mla, v_mla, main, tk=512, heads_per_step=2)
    return _out_proj(mix_a, mix_b, w_out.astype(BF16), x, post_norm_w.reshape(1, -1), tm=512)


def kernel(x, positions, pre_norm_w, w_in, q_norm_w, w_q_up, kv_norm_w, w_kv_up, w_out, post_norm_w):
    batch, depth = x.shape[0], pre_norm_w.shape[0]
    outs = []
    for b in range(batch):
        xb = x[b]
        cos, sin = _rope_tables(positions[b])
        for i in range(depth):
            xb = _layer(xb, cos, sin, pre_norm_w[i], w_in[i], q_norm_w[i], w_q_up[i],
                        kv_norm_w[i], w_kv_up[i], w_out[i], post_norm_w[i])
        outs.append(xb)
    return outs[0][None] if batch == 1 else jnp.stack(outs, axis=0)
```

```python
import functools
import math

import numpy as np
import jax
import jax.numpy as jnp
from jax import lax
from jax.experimental import pallas as pl
from jax.experimental.pallas import tpu as pltpu

F32 = jnp.float32
BF16 = jnp.bfloat16

LANES = 128
VMEM_LIMIT_BYTES = 56 * 1024 * 1024

CHUNK = 64
EPS = 1e-6
SB_HEADS = 8
SB_HEAD_DIM = 128
SB_WIDTH = SB_HEADS * SB_HEAD_DIM
MLA_HEADS = 8
MLA_NOPE_DIM = 128
MLA_ROPE_DIM = 64
MLA_QK_DIM = MLA_NOPE_DIM + MLA_ROPE_DIM
MLA_V_DIM = 128
MLA_WIDTH = MLA_HEADS * MLA_V_DIM
Q_LORA_RANK = 512
KV_LORA_RANK = 256
ROPE_THETA = 10000.0
HALF_ROPE = MLA_ROPE_DIM // 2
MLA_QK_PAD = 2 * LANES
LOG2_E = math.log2(math.e)
SB_SKIP_LOG2 = -150.0

MAIN_WIDTH = 4 * SB_WIDTH + MLA_WIDTH
TAIL_WIDTH = Q_LORA_RANK + KV_LORA_RANK + 2 * LANES
MLA_G_BLK = 4 * SB_WIDTH // LANES


def _rms_norm(x, w):
    return x * lax.rsqrt(jnp.mean(x * x, axis=-1, keepdims=True) + EPS) * w


def _silu(g):
    return g * (1.0 / (1.0 + jnp.exp(-g)))


def _dot_nt(a, b):
    return lax.dot_general(a, b, (((1,), (1,)), ((), ())), preferred_element_type=F32)


IN_TILE = 512
N_SB = 4 * SB_WIDTH // IN_TILE
N_MAIN = MAIN_WIDTH // IN_TILE
N_IN_TILES = N_MAIN + TAIL_WIDTH // IN_TILE
C_Q_ROW = 4 * SB_WIDTH
C_KV_ROW = C_Q_ROW + Q_LORA_RANK
MLA_GATE_ROW = C_KV_ROW + KV_LORA_RANK + MLA_ROPE_DIM
assert Q_LORA_RANK == IN_TILE and MLA_WIDTH == 2 * IN_TILE


def _in_proj_kernel(x_ref, nw_ref, wt_ref, main_ref, tail_ref, h_ref, *, q_scale):
    j = pl.program_id(1)

    @pl.when(j == 0)
    def _():
        h_ref[...] = _rms_norm(x_ref[...], nw_ref[...]).astype(BF16)

    def proj():
        return _dot_nt(h_ref[...], wt_ref[...].astype(BF16))

    @pl.when(j < SB_WIDTH // IN_TILE)
    def _():
        main_ref[...] = (proj() * q_scale).astype(BF16)

    @pl.when((j >= SB_WIDTH // IN_TILE) & (j < N_MAIN))
    def _():
        main_ref[...] = proj().astype(BF16)

    @pl.when(j >= N_MAIN)
    def _():
        tail_ref[...] = proj()


def _in_proj(x, norm_w, w_in_t, *, tm):
    s, d = x.shape
    kern = functools.partial(_in_proj_kernel, q_scale=LOG2_E / math.sqrt(SB_HEAD_DIM))

    def weight_row(i, j):
        unit = MLA_ROPE_DIM
        step = IN_TILE // unit
        row = jnp.where(j < N_SB, j * step,
                        jnp.where(j < N_MAIN, MLA_GATE_ROW // unit + (j - N_SB) * step,
                                  jnp.where(j == N_MAIN, C_Q_ROW // unit, C_KV_ROW // unit)))
        return (row * unit, 0)

    return pl.pallas_call(
        kern,
        grid=(s // tm, N_IN_TILES),
        in_specs=[
            pl.BlockSpec((tm, d), lambda i, j: (i, 0)),
            pl.BlockSpec((1, d), lambda i, j: (0, 0)),
            pl.BlockSpec((pl.Element(IN_TILE), pl.Element(d)), weight_row),
        ],
        out_specs=[
            pl.BlockSpec((tm, IN_TILE), lambda i, j: (i, jnp.minimum(j, N_MAIN - 1))),
            pl.BlockSpec((tm, IN_TILE), lambda i, j: (i, jnp.maximum(j - N_MAIN, 0))),
        ],
        out_shape=[jax.ShapeDtypeStruct((s, MAIN_WIDTH), BF16),
                   jax.ShapeDtypeStruct((s, TAIL_WIDTH), F32)],
        scratch_shapes=[pltpu.VMEM((tm, d), BF16)],
        compiler_params=pltpu.CompilerParams(
            dimension_semantics=("arbitrary", "arbitrary"), vmem_limit_bytes=VMEM_LIMIT_BYTES),
        name="in_proj",
    )(x, norm_w, w_in_t)


def _rope_table_kernel(pos_ref, freq_ref, cos_ref, sin_ref):
    ang = freq_ref[...] * pos_ref[...].astype(F32)
    zeros = jnp.zeros((LANES - MLA_ROPE_DIM, ang.shape[1]), F32)
    cos = jnp.cos(ang)
    sin = jnp.sin(ang)
    cos_ref[...] = jnp.concatenate([cos, cos, zeros], axis=0)
    sin_ref[...] = jnp.concatenate([sin, sin, zeros], axis=0)


def _rope_tables(positions):
    s = positions.shape[0]
    inv_freq = ROPE_THETA ** (-jnp.arange(0, MLA_ROPE_DIM, 2, dtype=F32) / MLA_ROPE_DIM)
    shape = jax.ShapeDtypeStruct((LANES, s), F32)
    return pl.pallas_call(_rope_table_kernel, out_shape=[shape, shape], name="rope_tab")(
        positions.reshape(1, s), inv_freq.reshape(HALF_ROPE, 1))


def _mla_prep_kernel(tail_ref, cos_ref, sin_ref, qnw_ref, kvnw_ref, wq_ref, wqr_ref, wkv_ref,
                     q_ref, k_ref, v_ref, *, q_scale):
    cos = cos_ref[...].T
    sin = sin_ref[...].T
    c_q = tail_ref[:, 0:Q_LORA_RANK]
    hq = _rms_norm(c_q, qnw_ref[...]).astype(BF16)
    zq = jnp.dot(hq, wq_ref[...], preferred_element_type=F32)
    zr = jnp.dot(hq, wqr_ref[...], preferred_element_type=F32)
    for h in range(MLA_HEADS):
        lo = h * MLA_QK_PAD
        q_ref[:, lo:lo + LANES] = (zq[:, lo:lo + LANES] * q_scale).astype(BF16)
        rot = zq[:, lo + LANES:lo + 2 * LANES] * cos + zr[:, h * LANES:(h + 1) * LANES] * sin
        q_ref[:, lo + LANES:lo + 2 * LANES] = (rot * q_scale).astype(BF16)

    c_kv = tail_ref[:, Q_LORA_RANK:Q_LORA_RANK + KV_LORA_RANK]
    hkv = _rms_norm(c_kv, kvnw_ref[...]).astype(BF16)
    zkv = jnp.dot(hkv, wkv_ref[...], preferred_element_type=F32)
    kr0 = Q_LORA_RANK + KV_LORA_RANK
    kr = tail_ref[:, kr0:kr0 + LANES]
    lane = lax.broadcasted_iota(jnp.int32, kr.shape, 1)
    rotated = jnp.where(lane < HALF_ROPE,
                        -pltpu.roll(kr, LANES - HALF_ROPE, axis=1),
                        pltpu.roll(kr, HALF_ROPE, axis=1))
    k_rot = (kr * cos + rotated * sin).astype(BF16)
    for h in range(MLA_HEADS):
        lo = h * MLA_QK_PAD
        k_ref[:, lo:lo + LANES] = zkv[:, h * LANES:(h + 1) * LANES].astype(BF16)
        k_ref[:, lo + LANES:lo + 2 * LANES] = k_rot
    v_ref[...] = zkv[:, MLA_HEADS * MLA_NOPE_DIM:].astype(BF16)


def _mla_prep(tail, cos, sin, q_norm_w, kv_norm_w, wq, wqr, wkv, *, tm):
    s = tail.shape[0]
    row = lambda i: (i, 0)
    fixed = lambda i: (0, 0)
    kern = functools.partial(_mla_prep_kernel, q_scale=LOG2_E / math.sqrt(MLA_QK_DIM))
    return pl.pallas_call(
        kern,
        grid=(s // tm,),
        in_specs=[
            pl.BlockSpec((tm, TAIL_WIDTH), row),
            pl.BlockSpec((LANES, tm), lambda i: (0, i)),
            pl.BlockSpec((LANES, tm), lambda i: (0, i)),
            pl.BlockSpec(q_norm_w.shape, fixed),
            pl.BlockSpec(kv_norm_w.shape, fixed),
            pl.BlockSpec(wq.shape, fixed),
            pl.BlockSpec(wqr.shape, fixed),
            pl.BlockSpec(wkv.shape, fixed),
        ],
        out_specs=[
            pl.BlockSpec((tm, MLA_HEADS * MLA_QK_PAD), row),
            pl.BlockSpec((tm, MLA_HEADS * MLA_QK_PAD), row),
            pl.BlockSpec((tm, MLA_WIDTH), row),
        ],
        out_shape=[jax.ShapeDtypeStruct((s, MLA_HEADS * MLA_QK_PAD), BF16),
                   jax.ShapeDtypeStruct((s, MLA_HEADS * MLA_QK_PAD), BF16),
                   jax.ShapeDtypeStruct((s, MLA_WIDTH), BF16)],
        compiler_params=pltpu.CompilerParams(
            dimension_semantics=("arbitrary",), vmem_limit_bytes=VMEM_LIMIT_BYTES),
        name="mla_prep",
    )(tail, cos, sin, q_norm_w, kv_norm_w, wq, wqr, wkv)


def _sb_attn_kernel(q_ref, k_ref, v_ref, g_ref, o_ref, tri_ref, acc_ref, carry_ref, *, tq, tk):
    i = pl.program_id(1)
    n_heads = q_ref.shape[1] // SB_HEAD_DIM

    rows = lax.broadcasted_iota(jnp.int32, (tk, tk), 0)
    cols = lax.broadcasted_iota(jnp.int32, (tk, tk), 1)
    tri = jnp.where(rows >= cols, -1.0, 0.0).astype(BF16)
    tri_ref[0:tk, :] = tri
    tri_ref[tk:2 * tk, :] = tri
    acc_ref[...] = jnp.zeros_like(acc_ref)
    carry_ref[...] = jnp.zeros_like(carry_ref)

    def block(start, masked):
        heads = range(n_heads)
        lanes = [slice(h * SB_HEAD_DIM, (h + 1) * SB_HEAD_DIM) for h in heads]
        z = [_dot_nt(q_ref[:, lanes[h]], k_ref[pl.ds(start, tk), lanes[h]]) for h in heads]
        sp = [jnp.maximum(z[h], 0.0) + jnp.log2(1.0 + jnp.exp2(-jnp.abs(z[h]))) for h in heads]
        if masked:
            t_idx = i * tq + lax.broadcasted_iota(jnp.int32, (tq, tk), 0)
            s_idx = start + lax.broadcasted_iota(jnp.int32, (tq, tk), 1)
            before = s_idx < t_idx
            sp = [jnp.where(before, sp[h], 0.0) for h in heads]
        tri = tri_ref[...]
        sums = []
        for h in heads:
            hi = sp[h].astype(BF16)
            lo = (sp[h] - hi.astype(F32)).astype(BF16)
            sums.append(jnp.dot(jnp.concatenate([hi, lo], axis=1), tri,
                                preferred_element_type=F32))
        carry = [carry_ref[:, lanes[h]] for h in heads]
        a = [jnp.exp2(z[h] + sums[h] + jnp.concatenate([carry[h]] * (tk // LANES), axis=1))
             for h in heads]
        if masked:
            a = [jnp.where(before, a[h], 0.0) for h in heads]
        for h in heads:
            acc_ref[:, lanes[h]] += jnp.dot(a[h].astype(BF16), v_ref[pl.ds(start, tk), lanes[h]],
                                            preferred_element_type=F32)
            carry_ref[:, lanes[h]] = carry[h] + jnp.broadcast_to(sums[h][:, 0:1], (tq, LANES))

    n_diag = tq // tk
    for d in reversed(range(n_diag)):
        block(pl.multiple_of(i * tq + d * tk, tk), True)

    def live():
        return (jnp.max(carry_ref[...]) > SB_SKIP_LOG2).astype(jnp.int32)

    def cond(state):
        j, alive = state
        return (j >= 0) & (alive > 0)

    def body(state):
        j, _ = state
        block(pl.multiple_of(j * tk, tk), False)
        return j - 1, live()

    lax.while_loop(cond, body, (i * n_diag - 1, live()))
    o_ref[...] = (acc_ref[...] * _silu(g_ref[...].astype(F32))).astype(o_ref.dtype)


def _sb_attn(main, *, tq, tk, heads_per_step):
    s = main.shape[0]
    n_groups = SB_HEADS // heads_per_step
    width = heads_per_step * SB_HEAD_DIM
    kern = functools.partial(_sb_attn_kernel, tq=tq, tk=tk)
    resident = pl.Buffered(1) if n_groups == 1 else None
    return pl.pallas_call(
        kern,
        grid=(n_groups, s // tq),
        in_specs=[
            pl.BlockSpec((tq, width), lambda g, i: (i, g)),
            pl.BlockSpec((s, width), lambda g, i: (0, n_groups + g), pipeline_mode=resident),
            pl.BlockSpec((s, width), lambda g, i: (0, 2 * n_groups + g), pipeline_mode=resident),
            pl.BlockSpec((tq, width), lambda g, i: (i, 3 * n_groups + g)),
        ],
        out_specs=pl.BlockSpec((tq, width), lambda g, i: (i, g)),
        out_shape=jax.ShapeDtypeStruct((s, SB_WIDTH), BF16),
        scratch_shapes=[pltpu.VMEM((2 * tk, tk), BF16),
                        pltpu.VMEM((tq, width), F32),
                        pltpu.VMEM((tq, width), F32)],
        compiler_params=pltpu.CompilerParams(
            dimension_semantics=("arbitrary", "arbitrary"), vmem_limit_bytes=VMEM_LIMIT_BYTES),
        name="sb_attn",
    )(main, main, main, main)


def _mla_attn_kernel(q_ref, k_ref, v_ref, g_ref, o_ref, s_ref, m_ref, acc_ref, *, tk):
    i = pl.program_id(1)
    tq = 2 * tk
    n_col = tk // LANES
    n_heads = v_ref.shape[1] // MLA_V_DIM
    heads = range(n_heads)
    row_tile = 256
    top, bottom, both = slice(0, tk), slice(tk, tq), slice(0, tq)
    ones = jnp.ones((tk, LANES), BF16)
    m_ref[...] = jnp.full_like(m_ref, -jnp.inf)
    acc_ref[...] = jnp.zeros_like(acc_ref)

    def qk_lanes(head):
        return slice(head * MLA_QK_PAD, (head + 1) * MLA_QK_PAD)

    def v_lanes(head):
        return slice(head * MLA_V_DIM, (head + 1) * MLA_V_DIM)

    def scores(head, blk, slot, rows):
        start = pl.multiple_of(blk * tk, tk)
        s_ref[head, slot, rows] = _dot_nt(q_ref[rows, qk_lanes(head)],
                                          k_ref[pl.ds(start, tk), qk_lanes(head)])

    def tiles(blk, slot, rows, masked, next_blk=None):
        return [(head, blk, slot, slice(r0, r0 + row_tile), r0 - rows.start, masked, next_blk)
                for r0 in range(rows.start, rows.stop, row_tile) for head in heads]

    def softmax_stage(item):
        head, blk, slot, rows, row_offset, masked, _ = item

        def column(c):
            s = s_ref[head, slot, rows, c * LANES:(c + 1) * LANES]
            if masked:
                t_chunk = (row_offset + lax.broadcasted_iota(jnp.int32, (row_tile, LANES), 0)) // CHUNK
                s_chunk = (c * LANES + lax.broadcasted_iota(jnp.int32, (row_tile, LANES), 1)) // CHUNK
                s = jnp.where(s_chunk <= t_chunk, s, -jnp.inf)
            return s

        col_max = column(0)
        for c in range(1, n_col):
            col_max = jnp.maximum(col_max, column(c))
        m_prev = m_ref[head, rows]
        m_new = jnp.maximum(m_prev, jnp.max(col_max, axis=-1, keepdims=True))
        m_ref[head, rows] = m_new
        p = jnp.concatenate([jnp.exp2((column(c) - m_new).astype(BF16)) for c in range(n_col)],
                            axis=1)
        return head, blk, rows, jnp.exp2(m_prev - m_new), p

    def value_stage(head, blk, rows, alpha, p):
        start = pl.multiple_of(blk * tk, tk)
        v_ones = jnp.concatenate([v_ref[pl.ds(start, tk), v_lanes(head)], ones], axis=1)
        acc_ref[head, rows] = (jnp.concatenate([alpha, alpha], axis=1) * acc_ref[head, rows]
                               + jnp.dot(p, v_ones, preferred_element_type=F32))

    def fold(items):
        for item in items:
            head, _, slot, rows, _, _, next_blk = item
            if next_blk is not None:
                scores(head, next_blk, 1 - slot, rows)
            value_stage(*softmax_stage(item))

    for head in heads:
        scores(head, 0, 0, both)

    def pair(n, _):
        blk = 2 * n
        fold(tiles(blk, 0, both, False, next_blk=blk + 1)
             + tiles(blk + 1, 1, both, False, next_blk=blk + 2))
        return 0

    lax.fori_loop(0, i, pair, 0)

    for head in heads:
        scores(head, 2 * i + 1, 1, bottom)
    fold(tiles(2 * i, 0, top, True) + tiles(2 * i, 0, bottom, False)
         + tiles(2 * i + 1, 1, bottom, True))

    for head in heads:
        out = acc_ref[head, :, 0:MLA_V_DIM] * (1.0 / acc_ref[head, :, MLA_V_DIM:])
        gate = g_ref[:, v_lanes(head)].astype(F32)
        o_ref[:, v_lanes(head)] = (out * _silu(gate)).astype(o_ref.dtype)


def _mla_attn(q, k, v, main, *, tk, heads_per_step):
    s = q.shape[0]
    tq = 2 * tk
    n_groups = MLA_HEADS // heads_per_step
    qk_width = heads_per_step * MLA_QK_PAD
    v_width = heads_per_step * MLA_V_DIM
    gate_block = MLA_G_BLK // heads_per_step
    kern = functools.partial(_mla_attn_kernel, tk=tk)
    return pl.pallas_call(
        kern,
        grid=(n_groups, s // tq),
        in_specs=[
            pl.BlockSpec((tq, qk_width), lambda g, i: (i, g)),
            pl.BlockSpec((s, qk_width), lambda g, i: (0, g)),
            pl.BlockSpec((s, v_width), lambda g, i: (0, g)),
            pl.BlockSpec((tq, v_width), lambda g, i: (i, gate_block + g)),
        ],
        out_specs=pl.BlockSpec((tq, v_width), lambda g, i: (i, g)),
        out_shape=jax.ShapeDtypeStruct((s, MLA_WIDTH), BF16),
        scratch_shapes=[pltpu.VMEM((heads_per_step, 2, tq, tk), F32),
                        pltpu.VMEM((heads_per_step, tq, LANES), F32),
                        pltpu.VMEM((heads_per_step, tq, MLA_V_DIM + LANES), F32)],
        compiler_params=pltpu.CompilerParams(
            dimension_semantics=("arbitrary", "arbitrary"), vmem_limit_bytes=VMEM_LIMIT_BYTES),
        name="mla_attn",
    )(q, k, v, main)


def _out_proj_kernel(a_ref, b_ref, w_ref, x_ref, nw_ref, o_ref, wb_ref):
    @pl.when(pl.program_id(0) == 0)
    def _():
        wb_ref[...] = w_ref[...].astype(BF16)

    y = (jnp.dot(a_ref[...], wb_ref[0:SB_WIDTH, :], preferred_element_type=F32)
         + jnp.dot(b_ref[...], wb_ref[SB_WIDTH:, :], preferred_element_type=F32))
    o_ref[...] = x_ref[...] + _rms_norm(y, nw_ref[...])


def _out_proj(mix_a, mix_b, w_out, x, norm_w, *, tm):
    s, d = x.shape
    row = lambda i: (i, 0)
    fixed = lambda i: (0, 0)
    return pl.pallas_call(
        _out_proj_kernel,
        grid=(s // tm,),
        in_specs=[
            pl.BlockSpec((tm, SB_WIDTH), row),
            pl.BlockSpec((tm, MLA_WIDTH), row),
            pl.BlockSpec(w_out.shape, fixed, pipeline_mode=pl.Buffered(1)),
            pl.BlockSpec((tm, d), row),
            pl.BlockSpec((1, d), fixed),
        ],
        out_specs=pl.BlockSpec((tm, d), row),
        out_shape=jax.ShapeDtypeStruct((s, d), F32),
        scratch_shapes=[pltpu.VMEM(w_out.shape, BF16)],
        compiler_params=pltpu.CompilerParams(
            dimension_semantics=("arbitrary",), vmem_limit_bytes=VMEM_LIMIT_BYTES),
        name="out_proj",
    )(mix_a, mix_b, w_out, x, norm_w)


def _rotate_half_cols(w):
    w1, w2 = jnp.split(w, 2, axis=-1)
    return jnp.concatenate([-w2, w1], axis=-1)


def _prep_q_up_weight(w_q_up):
    r = w_q_up.shape[0]
    w = w_q_up.reshape(r, MLA_HEADS, MLA_QK_DIM)
    nope, rope = w[:, :, :MLA_NOPE_DIM], w[:, :, MLA_NOPE_DIM:]
    pad = jnp.zeros((r, MLA_HEADS, LANES - MLA_ROPE_DIM), w.dtype)
    wq = jnp.concatenate([nope, rope, pad], axis=2).reshape(r, MLA_HEADS * MLA_QK_PAD)
    wqr = jnp.concatenate([_rotate_half_cols(rope), pad], axis=2).reshape(r, MLA_HEADS * LANES)
    return wq.astype(BF16), wqr.astype(BF16)


def _prep_kv_up_weight(w_kv_up):
    r = w_kv_up.shape[0]
    w = w_kv_up.reshape(r, MLA_HEADS, MLA_NOPE_DIM + MLA_V_DIM)
    k_nope = w[:, :, :MLA_NOPE_DIM].reshape(r, MLA_HEADS * MLA_NOPE_DIM)
    v = w[:, :, MLA_NOPE_DIM:].reshape(r, MLA_WIDTH)
    return jnp.concatenate([k_nope, v], axis=1).astype(BF16)


def _layer(x, cos, sin, pre_norm_w, w_in, q_norm_w, w_q_up, kv_norm_w, w_kv_up, w_out, post_norm_w):
    main, tail = _in_proj(x, pre_norm_w.reshape(1, -1), w_in.T, tm=1024)
    wq, wqr = _prep_q_up_weight(w_q_up)
    q_mla, k_mla, v_mla = _mla_prep(tail, cos, sin, q_norm_w.reshape(1, -1), kv_norm_w.reshape(1, -1),
                                    wq, wqr, _prep_kv_up_weight(w_kv_up), tm=512)
    mix_a = _sb_attn(main, tq=256, tk=256, heads_per_step=8)
    mix_b = _mla_attn(q_mla, k_mla, v_mla, main, tk=512, heads_per_step=2)
    return _out_proj(mix_a, mix_b, w_out, x, post_norm_w.reshape(1, -1), tm=512)


def kernel(x, positions, pre_norm_w, w_in, q_norm_w, w_q_up, kv_norm_w, w_kv_up, w_out, post_norm_w):
    batch, depth = x.shape[0], pre_norm_w.shape[0]
    outs = []
    for b in range(batch):
        xb = x[b]
        cos, sin = _rope_tables(positions[b])
        for i in range(depth):
            xb = _layer(xb, cos, sin, pre_norm_w[i], w_in[i], q_norm_w[i], w_q_up[i],
                        kv_norm_w[i], w_kv_up[i], w_out[i], post_norm_w[i])
        outs.append(xb)
    return outs[0][None] if batch == 1 else jnp.stack(outs, axis=0)
```

```python
import functools
import math

import numpy as np
import jax
import jax.numpy as jnp
from jax import lax
from jax.experimental import pallas as pl
from jax.experimental.pallas import tpu as pltpu

F32 = jnp.float32
BF16 = jnp.bfloat16

LANES = 128
VMEM_LIMIT_BYTES = 56 * 1024 * 1024

CHUNK = 64
EPS = 1e-6
SB_HEADS = 8
SB_HEAD_DIM = 128
SB_WIDTH = SB_HEADS * SB_HEAD_DIM
MLA_HEADS = 8
MLA_NOPE_DIM = 128
MLA_ROPE_DIM = 64
MLA_QK_DIM = MLA_NOPE_DIM + MLA_ROPE_DIM
MLA_V_DIM = 128
MLA_WIDTH = MLA_HEADS * MLA_V_DIM
Q_LORA_RANK = 512
KV_LORA_RANK = 256
ROPE_THETA = 10000.0
HALF_ROPE = MLA_ROPE_DIM // 2
MLA_QK_PAD = 2 * LANES
LOG2_E = math.log2(math.e)
SB_SKIP_LOG2 = -150.0

MAIN_WIDTH = 4 * SB_WIDTH + MLA_WIDTH
TAIL_WIDTH = Q_LORA_RANK + KV_LORA_RANK + 2 * LANES
MLA_G_BLK = 4 * SB_WIDTH // LANES


def _rms_norm(x, w):
    return x * lax.rsqrt(jnp.mean(x * x, axis=-1, keepdims=True) + EPS) * w


def _silu(g):
    return g * (1.0 / (1.0 + jnp.exp(-g)))


def _dot_nt(a, b):
    return lax.dot_general(a, b, (((1,), (1,)), ((), ())), preferred_element_type=F32)


IN_TILE = 1024
N_SB = 4 * SB_WIDTH // IN_TILE
N_MAIN = MAIN_WIDTH // IN_TILE
N_IN_TILES = N_MAIN + TAIL_WIDTH // IN_TILE
C_Q_ROW = 4 * SB_WIDTH
MLA_GATE_ROW = C_Q_ROW + Q_LORA_RANK + KV_LORA_RANK + MLA_ROPE_DIM
assert SB_WIDTH % IN_TILE == 0 and MLA_WIDTH % IN_TILE == 0 and TAIL_WIDTH % IN_TILE == 0


def _in_proj_kernel(x_ref, nw_ref, wt_ref, main_ref, tail_ref, h_ref, *, q_scale):
    j = pl.program_id(1)

    @pl.when(j == 0)
    def _():
        h_ref[...] = _rms_norm(x_ref[...], nw_ref[...]).astype(BF16)

    def proj():
        return _dot_nt(h_ref[...], wt_ref[...].astype(BF16))

    @pl.when(j < SB_WIDTH // IN_TILE)
    def _():
        main_ref[...] = (proj() * q_scale).astype(BF16)

    @pl.when((j >= SB_WIDTH // IN_TILE) & (j < N_MAIN))
    def _():
        main_ref[...] = proj().astype(BF16)

    @pl.when(j >= N_MAIN)
    def _():
        tail_ref[...] = proj()


def _in_proj(x, norm_w, w_in_t, *, tm):
    s, d = x.shape
    kern = functools.partial(_in_proj_kernel, q_scale=LOG2_E / math.sqrt(SB_HEAD_DIM))

    def weight_row(i, j):
        unit = MLA_ROPE_DIM
        step = IN_TILE // unit
        row = jnp.where(j < N_SB, j * step,
                        jnp.where(j < N_MAIN, MLA_GATE_ROW // unit + (j - N_SB) * step,
                                  C_Q_ROW // unit + (j - N_MAIN) * step))
        return (row * unit, 0)

    return pl.pallas_call(
        kern,
        grid=(s // tm, N_IN_TILES),
        in_specs=[
            pl.BlockSpec((tm, d), lambda i, j: (i, 0)),
            pl.BlockSpec((1, d), lambda i, j: (0, 0)),
            pl.BlockSpec((pl.Element(IN_TILE), pl.Element(d)), weight_row),
        ],
        out_specs=[
            pl.BlockSpec((tm, IN_TILE), lambda i, j: (i, jnp.minimum(j, N_MAIN - 1))),
            pl.BlockSpec((tm, IN_TILE), lambda i, j: (i, jnp.maximum(j - N_MAIN, 0))),
        ],
        out_shape=[jax.ShapeDtypeStruct((s, MAIN_WIDTH), BF16),
                   jax.ShapeDtypeStruct((s, TAIL_WIDTH), F32)],
        scratch_shapes=[pltpu.VMEM((tm, d), BF16)],
        compiler_params=pltpu.CompilerParams(
            dimension_semantics=("arbitrary", "arbitrary"), vmem_limit_bytes=VMEM_LIMIT_BYTES),
        name="in_proj",
    )(x, norm_w, w_in_t)


def _rope_table_kernel(pos_ref, freq_ref, cos_ref, sin_ref):
    ang = freq_ref[...] * pos_ref[...].astype(F32)
    zeros = jnp.zeros((LANES - MLA_ROPE_DIM, ang.shape[1]), F32)
    cos = jnp.cos(ang)
    sin = jnp.sin(ang)
    cos_ref[...] = jnp.concatenate([cos, cos, zeros], axis=0)
    sin_ref[...] = jnp.concatenate([sin, sin, zeros], axis=0)


def _rope_tables(positions):
    s = positions.shape[0]
    inv_freq = ROPE_THETA ** (-jnp.arange(0, MLA_ROPE_DIM, 2, dtype=F32) / MLA_ROPE_DIM)
    shape = jax.ShapeDtypeStruct((LANES, s), F32)
    return pl.pallas_call(_rope_table_kernel, out_shape=[shape, shape], name="rope_tab")(
        positions.reshape(1, s), inv_freq.reshape(HALF_ROPE, 1))


def _mla_prep_kernel(tail_ref, cos_ref, sin_ref, qnw_ref, kvnw_ref, wq_ref, wqr_ref, wkv_ref,
                     q_ref, k_ref, v_ref, *, q_scale):
    cos = cos_ref[...].T
    sin = sin_ref[...].T
    c_q = tail_ref[:, 0:Q_LORA_RANK]
    hq = _rms_norm(c_q, qnw_ref[...]).astype(BF16)
    zq = jnp.dot(hq, wq_ref[...], preferred_element_type=F32)
    zr = jnp.dot(hq, wqr_ref[...], preferred_element_type=F32)
    for h in range(MLA_HEADS):
        lo = h * MLA_QK_PAD
        q_ref[:, lo:lo + LANES] = (zq[:, lo:lo + LANES] * q_scale).astype(BF16)
        rot = zq[:, lo + LANES:lo + 2 * LANES] * cos + zr[:, h * LANES:(h + 1) * LANES] * sin
        q_ref[:, lo + LANES:lo + 2 * LANES] = (rot * q_scale).astype(BF16)

    c_kv = tail_ref[:, Q_LORA_RANK:Q_LORA_RANK + KV_LORA_RANK]
    hkv = _rms_norm(c_kv, kvnw_ref[...]).astype(BF16)
    zkv = jnp.dot(hkv, wkv_ref[...], preferred_element_type=F32)
    kr0 = Q_LORA_RANK + KV_LORA_RANK
    kr = tail_ref[:, kr0:kr0 + LANES]
    lane = lax.broadcasted_iota(jnp.int32, kr.shape, 1)
    rotated = jnp.where(lane < HALF_ROPE,
                        -pltpu.roll(kr, LANES - HALF_ROPE, axis=1),
                        pltpu.roll(kr, HALF_ROPE, axis=1))
    k_rot = (kr * cos + rotated * sin).astype(BF16)
    for h in range(MLA_HEADS):
        lo = h * MLA_QK_PAD
        k_ref[:, lo:lo + LANES] = zkv[:, h * LANES:(h + 1) * LANES].astype(BF16)
        k_ref[:, lo + LANES:lo + 2 * LANES] = k_rot
    v_ref[...] = zkv[:, MLA_HEADS * MLA_NOPE_DIM:].astype(BF16)


def _mla_prep(tail, cos, sin, q_norm_w, kv_norm_w, wq, wqr, wkv, *, tm):
    s = tail.shape[0]
    row = lambda i: (i, 0)
    fixed = lambda i: (0, 0)
    kern = functools.partial(_mla_prep_kernel, q_scale=LOG2_E / math.sqrt(MLA_QK_DIM))
    return pl.pallas_call(
        kern,
        grid=(s // tm,),
        in_specs=[
            pl.BlockSpec((tm, TAIL_WIDTH), row),
            pl.BlockSpec((LANES, tm), lambda i: (0, i)),
            pl.BlockSpec((LANES, tm), lambda i: (0, i)),
            pl.BlockSpec(q_norm_w.shape, fixed),
            pl.BlockSpec(kv_norm_w.shape, fixed),
            pl.BlockSpec(wq.shape, fixed),
            pl.BlockSpec(wqr.shape, fixed),
            pl.BlockSpec(wkv.shape, fixed),
        ],
        out_specs=[
            pl.BlockSpec((tm, MLA_HEADS * MLA_QK_PAD), row),
            pl.BlockSpec((tm, MLA_HEADS * MLA_QK_PAD), row),
            pl.BlockSpec((tm, MLA_WIDTH), row),
        ],
        out_shape=[jax.ShapeDtypeStruct((s, MLA_HEADS * MLA_QK_PAD), BF16),
                   jax.ShapeDtypeStruct((s, MLA_HEADS * MLA_QK_PAD), BF16),
                   jax.ShapeDtypeStruct((s, MLA_WIDTH), BF16)],
        compiler_params=pltpu.CompilerParams(
            dimension_semantics=("arbitrary",), vmem_limit_bytes=VMEM_LIMIT_BYTES),
        name="mla_prep",
    )(tail, cos, sin, q_norm_w, kv_norm_w, wq, wqr, wkv)


def _sb_attn_kernel(q_ref, k_ref, v_ref, g_ref, o_ref, tri_ref, acc_ref, carry_ref, *, tq, tk):
    i = pl.program_id(1)
    n_heads = q_ref.shape[1] // SB_HEAD_DIM

    rows = lax.broadcasted_iota(jnp.int32, (tk, tk), 0)
    cols = lax.broadcasted_iota(jnp.int32, (tk, tk), 1)
    tri = jnp.where(rows >= cols, -1.0, 0.0).astype(BF16)
    tri_ref[0:tk, :] = tri
    tri_ref[tk:2 * tk, :] = tri
    acc_ref[...] = jnp.zeros_like(acc_ref)
    carry_ref[...] = jnp.zeros_like(carry_ref)

    def block(start, masked):
        heads = range(n_heads)
        lanes = [slice(h * SB_HEAD_DIM, (h + 1) * SB_HEAD_DIM) for h in heads]
        z = [_dot_nt(q_ref[:, lanes[h]], k_ref[pl.ds(start, tk), lanes[h]]) for h in heads]
        sp = [jnp.maximum(z[h], 0.0) + jnp.log2(1.0 + jnp.exp2(-jnp.abs(z[h]))) for h in heads]
        if masked:
            t_idx = i * tq + lax.broadcasted_iota(jnp.int32, (tq, tk), 0)
            s_idx = start + lax.broadcasted_iota(jnp.int32, (tq, tk), 1)
            before = s_idx < t_idx
            sp = [jnp.where(before, sp[h], 0.0) for h in heads]
        tri = tri_ref[...]
        sums = []
        for h in heads:
            hi = sp[h].astype(BF16)
            lo = (sp[h] - hi.astype(F32)).astype(BF16)
            sums.append(jnp.dot(jnp.concatenate([hi, lo], axis=1), tri,
                                preferred_element_type=F32))
        carry = [carry_ref[:, lanes[h]] for h in heads]
        a = [jnp.exp2(z[h] + sums[h] + jnp.concatenate([carry[h]] * (tk // LANES), axis=1))
             for h in heads]
        if masked:
            a = [jnp.where(before, a[h], 0.0) for h in heads]
        for h in heads:
            acc_ref[:, lanes[h]] += jnp.dot(a[h].astype(BF16), v_ref[pl.ds(start, tk), lanes[h]],
                                            preferred_element_type=F32)
            carry_ref[:, lanes[h]] = carry[h] + jnp.broadcast_to(sums[h][:, 0:1], (tq, LANES))

    n_diag = tq // tk
    for d in reversed(range(n_diag)):
        block(pl.multiple_of(i * tq + d * tk, tk), True)

    def live():
        return (jnp.max(carry_ref[...]) > SB_SKIP_LOG2).astype(jnp.int32)

    def cond(state):
        j, alive = state
        return (j >= 0) & (alive > 0)

    def body(state):
        j, _ = state
        block(pl.multiple_of(j * tk, tk), False)
        return j - 1, live()

    lax.while_loop(cond, body, (i * n_diag - 1, live()))
    o_ref[...] = (acc_ref[...] * _silu(g_ref[...].astype(F32))).astype(o_ref.dtype)


def _sb_attn(main, *, tq, tk, heads_per_step):
    s = main.shape[0]
    n_groups = SB_HEADS // heads_per_step
    width = heads_per_step * SB_HEAD_DIM
    kern = functools.partial(_sb_attn_kernel, tq=tq, tk=tk)
    resident = pl.Buffered(1) if n_groups == 1 else None
    return pl.pallas_call(
        kern,
        grid=(n_groups, s // tq),
        in_specs=[
            pl.BlockSpec((tq, width), lambda g, i: (i, g)),
            pl.BlockSpec((s, width), lambda g, i: (0, n_groups + g), pipeline_mode=resident),
            pl.BlockSpec((s, width), lambda g, i: (0, 2 * n_groups + g), pipeline_mode=resident),
            pl.BlockSpec((tq, width), lambda g, i: (i, 3 * n_groups + g)),
        ],
        out_specs=pl.BlockSpec((tq, width), lambda g, i: (i, g)),
        out_shape=jax.ShapeDtypeStruct((s, SB_WIDTH), BF16),
        scratch_shapes=[pltpu.VMEM((2 * tk, tk), BF16),
                        pltpu.VMEM((tq, width), F32),
                        pltpu.VMEM((tq, width), F32)],
        compiler_params=pltpu.CompilerParams(
            dimension_semantics=("arbitrary", "arbitrary"), vmem_limit_bytes=VMEM_LIMIT_BYTES),
        name="sb_attn",
    )(main, main, main, main)


def _mla_attn_kernel(q_ref, k_ref, v_ref, g_ref, o_ref, s_ref, m_ref, acc_ref, *, tk):
    i = pl.program_id(1)
    tq = 2 * tk
    n_col = tk // LANES
    n_heads = v_ref.shape[1] // MLA_V_DIM
    heads = range(n_heads)
    row_tile = 256
    top, bottom, both = slice(0, tk), slice(tk, tq), slice(0, tq)
    ones = jnp.ones((tk, LANES), BF16)
    m_ref[...] = jnp.full_like(m_ref, -jnp.inf)
    acc_ref[...] = jnp.zeros_like(acc_ref)

    def qk_lanes(head):
        return slice(head * MLA_QK_PAD, (head + 1) * MLA_QK_PAD)

    def v_lanes(head):
        return slice(head * MLA_V_DIM, (head + 1) * MLA_V_DIM)

    def scores(head, blk, slot, rows):
        start = pl.multiple_of(blk * tk, tk)
        s_ref[head, slot, rows] = _dot_nt(q_ref[rows, qk_lanes(head)],
                                          k_ref[pl.ds(start, tk), qk_lanes(head)])

    def tiles(blk, slot, rows, masked, next_blk=None):
        return [(head, blk, slot, slice(r0, r0 + row_tile), r0 - rows.start, masked, next_blk)
                for r0 in range(rows.start, rows.stop, row_tile) for head in heads]

    def softmax_stage(item):
        head, blk, slot, rows, row_offset, masked, _ = item

        def column(c):
            s = s_ref[head, slot, rows, c * LANES:(c + 1) * LANES]
            if masked:
                t_chunk = (row_offset + lax.broadcasted_iota(jnp.int32, (row_tile, LANES), 0)) // CHUNK
                s_chunk = (c * LANES + lax.broadcasted_iota(jnp.int32, (row_tile, LANES), 1)) // CHUNK
                s = jnp.where(s_chunk <= t_chunk, s, -jnp.inf)
            return s

        col_max = column(0)
        for c in range(1, n_col):
            col_max = jnp.maximum(col_max, column(c))
        m_prev = m_ref[head, rows]
        m_new = jnp.maximum(m_prev, jnp.max(col_max, axis=-1, keepdims=True))
        m_ref[head, rows] = m_new
        p = jnp.concatenate([jnp.exp2((column(c) - m_new).astype(BF16)) for c in range(n_col)],
                            axis=1)
        return head, blk, rows, jnp.exp2(m_prev - m_new), p

    def value_stage(head, blk, rows, alpha, p):
        start = pl.multiple_of(blk * tk, tk)
        v_ones = jnp.concatenate([v_ref[pl.ds(start, tk), v_lanes(head)], ones], axis=1)
        acc_ref[head, rows] = (jnp.concatenate([alpha, alpha], axis=1) * acc_ref[head, rows]
                               + jnp.dot(p, v_ones, preferred_element_type=F32))

    def fold(items):
        for item in items:
            head, _, slot, rows, _, _, next_blk = item
            if next_blk is not None:
                scores(head, next_blk, 1 - slot, rows)
            value_stage(*softmax_stage(item))

    for head in heads:
        scores(head, 0, 0, both)

    def pair(n, _):
        blk = 2 * n
        fold(tiles(blk, 0, both, False, next_blk=blk + 1)
             + tiles(blk + 1, 1, both, False, next_blk=blk + 2))
        return 0

    lax.fori_loop(0, i, pair, 0)

    for head in heads:
        scores(head, 2 * i + 1, 1, bottom)
    fold(tiles(2 * i, 0, top, True) + tiles(2 * i, 0, bottom, False)
         + tiles(2 * i + 1, 1, bottom, True))

    for head in heads:
        out = acc_ref[head, :, 0:MLA_V_DIM] * (1.0 / acc_ref[head, :, MLA_V_DIM:])
        gate = g_ref[:, v_lanes(head)].astype(F32)
        o_ref[:, v_lanes(head)] = (out * _silu(gate)).astype(o_ref.dtype)


def _mla_attn(q, k, v, main, *, tk, heads_per_step):
    s = q.shape[0]
    tq = 2 * tk
    n_groups = MLA_HEADS // heads_per_step
    qk_width = heads_per_step * MLA_QK_PAD
    v_width = heads_per_step * MLA_V_DIM
    gate_block = MLA_G_BLK // heads_per_step
    kern = functools.partial(_mla_attn_kernel, tk=tk)
    return pl.pallas_call(
        kern,
        grid=(n_groups, s // tq),
        in_specs=[
            pl.BlockSpec((tq, qk_width), lambda g, i: (i, g)),
            pl.BlockSpec((s, qk_width), lambda g, i: (0, g)),
            pl.BlockSpec((s, v_width), lambda g, i: (0, g)),
            pl.BlockSpec((tq, v_width), lambda g, i: (i, gate_block + g)),
        ],
        out_specs=pl.BlockSpec((tq, v_width), lambda g, i: (i, g)),
        out_shape=jax.ShapeDtypeStruct((s, MLA_WIDTH), BF16),
        scratch_shapes=[pltpu.VMEM((heads_per_step, 2, tq, tk), F32),
                        pltpu.VMEM((heads_per_step, tq, LANES), F32),
                        pltpu.VMEM((heads_per_step, tq, MLA_V_DIM + LANES), F32)],
        compiler_params=pltpu.CompilerParams(
            dimension_semantics=("arbitrary", "arbitrary"), vmem_limit_bytes=VMEM_LIMIT_BYTES),
        name="mla_attn",
    )(q, k, v, main)


def _out_proj_kernel(a_ref, b_ref, w_ref, x_ref, nw_ref, o_ref, wb_ref):
    @pl.when(pl.program_id(0) == 0)
    def _():
        wb_ref[...] = w_ref[...].astype(BF16)

    y = (jnp.dot(a_ref[...], wb_ref[0:SB_WIDTH, :], preferred_element_type=F32)
         + jnp.dot(b_ref[...], wb_ref[SB_WIDTH:, :], preferred_element_type=F32))
    o_ref[...] = x_ref[...] + _rms_norm(y, nw_ref[...])


def _out_proj(mix_a, mix_b, w_out, x, norm_w, *, tm):
    s, d = x.shape
    row = lambda i: (i, 0)
    fixed = lambda i: (0, 0)
    return pl.pallas_call(
        _out_proj_kernel,
        grid=(s // tm,),
        in_specs=[
            pl.BlockSpec((tm, SB_WIDTH), row),
            pl.BlockSpec((tm, MLA_WIDTH), row),
            pl.BlockSpec(w_out.shape, fixed, pipeline_mode=pl.Buffered(1)),
            pl.BlockSpec((tm, d), row),
            pl.BlockSpec((1, d), fixed),
        ],
        out_specs=pl.BlockSpec((tm, d), row),
        out_shape=jax.ShapeDtypeStruct((s, d), F32),
        scratch_shapes=[pltpu.VMEM(w_out.shape, BF16)],
        compiler_params=pltpu.CompilerParams(
            dimension_semantics=("arbitrary",), vmem_limit_bytes=VMEM_LIMIT_BYTES),
        name="out_proj",
    )(mix_a, mix_b, w_out, x, norm_w)


def _rotate_half_cols(w):
    w1, w2 = jnp.split(w, 2, axis=-1)
    return jnp.concatenate([-w2, w1], axis=-1)


def _prep_q_up_weight(w_q_up):
    r = w_q_up.shape[0]
    w = w_q_up.reshape(r, MLA_HEADS, MLA_QK_DIM)
    nope, rope = w[:, :, :MLA_NOPE_DIM], w[:, :, MLA_NOPE_DIM:]
    pad = jnp.zeros((r, MLA_HEADS, LANES - MLA_ROPE_DIM), w.dtype)
    wq = jnp.concatenate([nope, rope, pad], axis=2).reshape(r, MLA_HEADS * MLA_QK_PAD)
    wqr = jnp.concatenate([_rotate_half_cols(rope), pad], axis=2).reshape(r, MLA_HEADS * LANES)
    return wq.astype(BF16), wqr.astype(BF16)


def _prep_kv_up_weight(w_kv_up):
    r = w_kv_up.shape[0]
    w = w_kv_up.reshape(r, MLA_HEADS, MLA_NOPE_DIM + MLA_V_DIM)
    k_nope = w[:, :, :MLA_NOPE_DIM].reshape(r, MLA_HEADS * MLA_NOPE_DIM)
    v = w[:, :, MLA_NOPE_DIM:].reshape(r, MLA_WIDTH)
    return jnp.concatenate([k_nope, v], axis=1).astype(BF16)


def _layer(x, cos, sin, pre_norm_w, w_in, q_norm_w, w_q_up, kv_norm_w, w_kv_up, w_out, post_norm_w):
    main, tail = _in_proj(x, pre_norm_w.reshape(1, -1), w_in.T, tm=1024)
    wq, wqr = _prep_q_up_weight(w_q_up)
    q_mla, k_mla, v_mla = _mla_prep(tail, cos, sin, q_norm_w.reshape(1, -1), kv_norm_w.reshape(1, -1),
                                    wq, wqr, _prep_kv_up_weight(w_kv_up), tm=512)
    mix_a = _sb_attn(main, tq=256, tk=256, heads_per_step=8)
    mix_b = _mla_attn(q_mla, k_mla, v_mla, main, tk=512, heads_per_step=2)
    return _out_proj(mix_a, mix_b, w_out, x, post_norm_w.reshape(1, -1), tm=512)


def kernel(x, positions, pre_norm_w, w_in, q_norm_w, w_q_up, kv_norm_w, w_kv_up, w_out, post_norm_w):
    batch, depth = x.shape[0], pre_norm_w.shape[0]
    outs = []
    for b in range(batch):
        xb = x[b]
        cos, sin = _rope_tables(positions[b])
        for i in range(depth):
            xb = _layer(xb, cos, sin, pre_norm_w[i], w_in[i], q_norm_w[i], w_q_up[i],
                        kv_norm_w[i], w_kv_up[i], w_out[i], post_norm_w[i])
        outs.append(xb)
    return outs[0][None] if batch == 1 else jnp.stack(outs, axis=0)
```

```python
import functools
import math

import numpy as np
import jax
import jax.numpy as jnp
from jax import lax
from jax.experimental import pallas as pl
from jax.experimental.pallas import tpu as pltpu

F32 = jnp.float32
BF16 = jnp.bfloat16

LANES = 128
VMEM_LIMIT_BYTES = 56 * 1024 * 1024

CHUNK = 64
EPS = 1e-6
SB_HEADS = 8
SB_HEAD_DIM = 128
SB_WIDTH = SB_HEADS * SB_HEAD_DIM
MLA_HEADS = 8
MLA_NOPE_DIM = 128
MLA_ROPE_DIM = 64
MLA_QK_DIM = MLA_NOPE_DIM + MLA_ROPE_DIM
MLA_V_DIM = 128
MLA_WIDTH = MLA_HEADS * MLA_V_DIM
Q_LORA_RANK = 512
KV_LORA_RANK = 256
ROPE_THETA = 10000.0
HALF_ROPE = MLA_ROPE_DIM // 2
MLA_QK_PAD = 2 * LANES
LOG2_E = math.log2(math.e)
SB_SKIP_LOG2 = -150.0

MAIN_WIDTH = 4 * SB_WIDTH + MLA_WIDTH
TAIL_WIDTH = Q_LORA_RANK + KV_LORA_RANK + 2 * LANES
MLA_G_BLK = 4 * SB_WIDTH // LANES


def _rms_norm(x, w):
    return x * lax.rsqrt(jnp.mean(x * x, axis=-1, keepdims=True) + EPS) * w


def _silu(g):
    return g * (1.0 / (1.0 + jnp.exp(-g)))


def _dot_nt(a, b):
    return lax.dot_general(a, b, (((1,), (1,)), ((), ())), preferred_element_type=F32)


IN_TILE = 1024
N_SB = 4 * SB_WIDTH // IN_TILE
N_MAIN = MAIN_WIDTH // IN_TILE
N_IN_TILES = N_MAIN + TAIL_WIDTH // IN_TILE
C_Q_ROW = 4 * SB_WIDTH
MLA_GATE_ROW = C_Q_ROW + Q_LORA_RANK + KV_LORA_RANK + MLA_ROPE_DIM
assert SB_WIDTH % IN_TILE == 0 and MLA_WIDTH % IN_TILE == 0 and TAIL_WIDTH % IN_TILE == 0


def _in_proj_kernel(x_ref, nw_ref, wt_ref, main_ref, tail_ref, h_ref, *, q_scale):
    j = pl.program_id(1)

    @pl.when(j == 0)
    def _():
        h_ref[...] = _rms_norm(x_ref[...], nw_ref[...]).astype(BF16)

    def proj():
        return _dot_nt(h_ref[...], wt_ref[...].astype(BF16))

    @pl.when(j < SB_WIDTH // IN_TILE)
    def _():
        main_ref[...] = (proj() * q_scale).astype(BF16)

    @pl.when((j >= SB_WIDTH // IN_TILE) & (j < N_MAIN))
    def _():
        main_ref[...] = proj().astype(BF16)

    @pl.when(j >= N_MAIN)
    def _():
        tail_ref[...] = proj()


def _in_proj(x, norm_w, w_in_t, *, tm):
    s, d = x.shape
    kern = functools.partial(_in_proj_kernel, q_scale=LOG2_E / math.sqrt(SB_HEAD_DIM))

    def weight_row(i, j):
        unit = MLA_ROPE_DIM
        step = IN_TILE // unit
        row = jnp.where(j < N_SB, j * step,
                        jnp.where(j < N_MAIN, MLA_GATE_ROW // unit + (j - N_SB) * step,
                                  C_Q_ROW // unit + (j - N_MAIN) * step))
        return (row * unit, 0)

    return pl.pallas_call(
        kern,
        grid=(s // tm, N_IN_TILES),
        in_specs=[
            pl.BlockSpec((tm, d), lambda i, j: (i, 0)),
            pl.BlockSpec((1, d), lambda i, j: (0, 0)),
            pl.BlockSpec((pl.Element(IN_TILE), pl.Element(d)), weight_row),
        ],
        out_specs=[
            pl.BlockSpec((tm, IN_TILE), lambda i, j: (i, jnp.minimum(j, N_MAIN - 1))),
            pl.BlockSpec((tm, IN_TILE), lambda i, j: (i, jnp.maximum(j - N_MAIN, 0))),
        ],
        out_shape=[jax.ShapeDtypeStruct((s, MAIN_WIDTH), BF16),
                   jax.ShapeDtypeStruct((s, TAIL_WIDTH), F32)],
        scratch_shapes=[pltpu.VMEM((tm, d), BF16)],
        compiler_params=pltpu.CompilerParams(
            dimension_semantics=("arbitrary", "arbitrary"), vmem_limit_bytes=VMEM_LIMIT_BYTES),
        name="in_proj",
    )(x, norm_w, w_in_t)


def _rope_table_kernel(pos_ref, freq_ref, cos_ref, sin_ref):
    ang = freq_ref[...] * pos_ref[...].astype(F32)
    zeros = jnp.zeros((LANES - MLA_ROPE_DIM, ang.shape[1]), F32)
    cos = jnp.cos(ang)
    sin = jnp.sin(ang)
    cos_ref[...] = jnp.concatenate([cos, cos, zeros], axis=0)
    sin_ref[...] = jnp.concatenate([sin, sin, zeros], axis=0)


def _rope_tables(positions):
    s = positions.shape[0]
    inv_freq = ROPE_THETA ** (-jnp.arange(0, MLA_ROPE_DIM, 2, dtype=F32) / MLA_ROPE_DIM)
    shape = jax.ShapeDtypeStruct((LANES, s), F32)
    return pl.pallas_call(_rope_table_kernel, out_shape=[shape, shape], name="rope_tab")(
        positions.reshape(1, s), inv_freq.reshape(HALF_ROPE, 1))


def _mla_prep_kernel(tail_ref, cos_ref, sin_ref, qnw_ref, kvnw_ref, wq_ref, wqr_ref, wkv_ref,
                     q_ref, k_ref, v_ref, *, q_scale):
    cos = cos_ref[...].T
    sin = sin_ref[...].T
    c_q = tail_ref[:, 0:Q_LORA_RANK]
    hq = _rms_norm(c_q, qnw_ref[...]).astype(BF16)
    zq = jnp.dot(hq, wq_ref[...], preferred_element_type=F32)
    zr = jnp.dot(hq, wqr_ref[...], preferred_element_type=F32)
    for h in range(MLA_HEADS):
        lo = h * MLA_QK_PAD
        q_ref[:, lo:lo + LANES] = (zq[:, lo:lo + LANES] * q_scale).astype(BF16)
        rot = zq[:, lo + LANES:lo + 2 * LANES] * cos + zr[:, h * LANES:(h + 1) * LANES] * sin
        q_ref[:, lo + LANES:lo + 2 * LANES] = (rot * q_scale).astype(BF16)

    c_kv = tail_ref[:, Q_LORA_RANK:Q_LORA_RANK + KV_LORA_RANK]
    hkv = _rms_norm(c_kv, kvnw_ref[...]).astype(BF16)
    zkv = jnp.dot(hkv, wkv_ref[...], preferred_element_type=F32)
    kr0 = Q_LORA_RANK + KV_LORA_RANK
    kr = tail_ref[:, kr0:kr0 + LANES]
    lane = lax.broadcasted_iota(jnp.int32, kr.shape, 1)
    rotated = jnp.where(lane < HALF_ROPE,
                        -pltpu.roll(kr, LANES - HALF_ROPE, axis=1),
                        pltpu.roll(kr, HALF_ROPE, axis=1))
    k_rot = (kr * cos + rotated * sin).astype(BF16)
    for h in range(MLA_HEADS):
        lo = h * MLA_QK_PAD
        k_ref[:, lo:lo + LANES] = zkv[:, h * LANES:(h + 1) * LANES].astype(BF16)
        k_ref[:, lo + LANES:lo + 2 * LANES] = k_rot
    v_ref[...] = zkv[:, MLA_HEADS * MLA_NOPE_DIM:].astype(BF16)


def _mla_prep(tail, cos, sin, q_norm_w, kv_norm_w, wq, wqr, wkv, *, tm):
    s = tail.shape[0]
    row = lambda i: (i, 0)
    fixed = lambda i: (0, 0)
    kern = functools.partial(_mla_prep_kernel, q_scale=LOG2_E / math.sqrt(MLA_QK_DIM))
    return pl.pallas_call(
        kern,
        grid=(s // tm,),
        in_specs=[
            pl.BlockSpec((tm, TAIL_WIDTH), row),
            pl.BlockSpec((LANES, tm), lambda i: (0, i)),
            pl.BlockSpec((LANES, tm), lambda i: (0, i)),
            pl.BlockSpec(q_norm_w.shape, fixed),
            pl.BlockSpec(kv_norm_w.shape, fixed),
            pl.BlockSpec(wq.shape, fixed),
            pl.BlockSpec(wqr.shape, fixed),
            pl.BlockSpec(wkv.shape, fixed),
        ],
        out_specs=[
            pl.BlockSpec((tm, MLA_HEADS * MLA_QK_PAD), row),
            pl.BlockSpec((tm, MLA_HEADS * MLA_QK_PAD), row),
            pl.BlockSpec((tm, MLA_WIDTH), row),
        ],
        out_shape=[jax.ShapeDtypeStruct((s, MLA_HEADS * MLA_QK_PAD), BF16),
                   jax.ShapeDtypeStruct((s, MLA_HEADS * MLA_QK_PAD), BF16),
                   jax.ShapeDtypeStruct((s, MLA_WIDTH), BF16)],
        compiler_params=pltpu.CompilerParams(
            dimension_semantics=("arbitrary",), vmem_limit_bytes=VMEM_LIMIT_BYTES),
        name="mla_prep",
    )(tail, cos, sin, q_norm_w, kv_norm_w, wq, wqr, wkv)


def _sb_attn_kernel(q_ref, k_ref, v_ref, g_ref, o_ref, tri_ref, acc_ref, carry_ref, *, tq, tk):
    i = pl.program_id(1)
    n_heads = q_ref.shape[1] // SB_HEAD_DIM

    rows = lax.broadcasted_iota(jnp.int32, (tk, tk), 0)
    cols = lax.broadcasted_iota(jnp.int32, (tk, tk), 1)
    tri = jnp.where(rows >= cols, -1.0, 0.0).astype(BF16)
    tri_ref[0:tk, :] = tri
    tri_ref[tk:2 * tk, :] = tri
    acc_ref[...] = jnp.zeros_like(acc_ref)
    carry_ref[...] = jnp.zeros_like(carry_ref)

    def block(start, masked):
        heads = range(n_heads)
        lanes = [slice(h * SB_HEAD_DIM, (h + 1) * SB_HEAD_DIM) for h in heads]
        z = [_dot_nt(q_ref[:, lanes[h]], k_ref[pl.ds(start, tk), lanes[h]]) for h in heads]
        if masked:
            t_idx = i * tq + lax.broadcasted_iota(jnp.int32, (tq, tk), 0)
            s_idx = start + lax.broadcasted_iota(jnp.int32, (tq, tk), 1)
            before = s_idx < t_idx
            z = [jnp.where(before, z[h], -jnp.inf) for h in heads]
        sp = [jnp.maximum(z[h], 0.0) + jnp.log2(1.0 + jnp.exp2(-jnp.abs(z[h]))) for h in heads]
        tri = tri_ref[...]
        sums = []
        for h in heads:
            hi = sp[h].astype(BF16)
            lo = (sp[h] - hi.astype(F32)).astype(BF16)
            sums.append(jnp.dot(jnp.concatenate([hi, lo], axis=1), tri,
                                preferred_element_type=F32))
        carry = [carry_ref[:, lanes[h]] for h in heads]
        a = [jnp.exp2(z[h] + sums[h] + jnp.concatenate([carry[h]] * (tk // LANES), axis=1))
             for h in heads]
        for h in heads:
            acc_ref[:, lanes[h]] += jnp.dot(a[h].astype(BF16), v_ref[pl.ds(start, tk), lanes[h]],
                                            preferred_element_type=F32)
            carry_ref[:, lanes[h]] = carry[h] + jnp.broadcast_to(sums[h][:, 0:1], (tq, LANES))

    n_diag = tq // tk
    for d in reversed(range(n_diag)):
        block(pl.multiple_of(i * tq + d * tk, tk), True)

    def live():
        return (jnp.max(carry_ref[...]) > SB_SKIP_LOG2).astype(jnp.int32)

    def cond(state):
        j, alive = state
        return (j >= 0) & (alive > 0)

    def body(state):
        j, _ = state
        block(pl.multiple_of(j * tk, tk), False)
        return j - 1, live()

    lax.while_loop(cond, body, (i * n_diag - 1, live()))
    o_ref[...] = (acc_ref[...] * _silu(g_ref[...].astype(F32))).astype(o_ref.dtype)


def _sb_attn(main, *, tq, tk, heads_per_step):
    s = main.shape[0]
    n_groups = SB_HEADS // heads_per_step
    width = heads_per_step * SB_HEAD_DIM
    kern = functools.partial(_sb_attn_kernel, tq=tq, tk=tk)
    resident = pl.Buffered(1) if n_groups == 1 else None
    return pl.pallas_call(
        kern,
        grid=(n_groups, s // tq),
        in_specs=[
            pl.BlockSpec((tq, width), lambda g, i: (i, g)),
            pl.BlockSpec((s, width), lambda g, i: (0, n_groups + g), pipeline_mode=resident),
            pl.BlockSpec((s, width), lambda g, i: (0, 2 * n_groups + g), pipeline_mode=resident),
            pl.BlockSpec((tq, width), lambda g, i: (i, 3 * n_groups + g)),
        ],
        out_specs=pl.BlockSpec((tq, width), lambda g, i: (i, g)),
        out_shape=jax.ShapeDtypeStruct((s, SB_WIDTH), BF16),
        scratch_shapes=[pltpu.VMEM((2 * tk, tk), BF16),
                        pltpu.VMEM((tq, width), F32),
                        pltpu.VMEM((tq, width), F32)],
        compiler_params=pltpu.CompilerParams(
            dimension_semantics=("arbitrary", "arbitrary"), vmem_limit_bytes=VMEM_LIMIT_BYTES),
        name="sb_attn",
    )(main, main, main, main)


def _mla_attn_kernel(q_ref, k_ref, v_ref, g_ref, o_ref, s_ref, m_ref, acc_ref, *, tk):
    i = pl.program_id(1)
    tq = 2 * tk
    n_col = tk // LANES
    n_heads = v_ref.shape[1] // MLA_V_DIM
    heads = range(n_heads)
    row_tile = 256
    top, bottom, both = slice(0, tk), slice(tk, tq), slice(0, tq)
    ones = jnp.ones((tk, LANES), BF16)
    m_ref[...] = jnp.full_like(m_ref, -jnp.inf)
    acc_ref[...] = jnp.zeros_like(acc_ref)

    def qk_lanes(head):
        return slice(head * MLA_QK_PAD, (head + 1) * MLA_QK_PAD)

    def v_lanes(head):
        return slice(head * MLA_V_DIM, (head + 1) * MLA_V_DIM)

    def scores(head, blk, slot, rows):
        start = pl.multiple_of(blk * tk, tk)
        s_ref[head, slot, rows] = _dot_nt(q_ref[rows, qk_lanes(head)],
                                          k_ref[pl.ds(start, tk), qk_lanes(head)])

    def tiles(blk, slot, rows, masked, next_blk=None):
        return [(head, blk, slot, slice(r0, r0 + row_tile), r0 - rows.start, masked, next_blk)
                for r0 in range(rows.start, rows.stop, row_tile) for head in heads]

    def softmax_stage(item):
        head, blk, slot, rows, row_offset, masked, _ = item

        def column(c):
            s = s_ref[head, slot, rows, c * LANES:(c + 1) * LANES]
            if masked:
                t_chunk = (row_offset + lax.broadcasted_iota(jnp.int32, (row_tile, LANES), 0)) // CHUNK
                s_chunk = (c * LANES + lax.broadcasted_iota(jnp.int32, (row_tile, LANES), 1)) // CHUNK
                s = jnp.where(s_chunk <= t_chunk, s, -jnp.inf)
            return s

        col_max = column(0)
        for c in range(1, n_col):
            col_max = jnp.maximum(col_max, column(c))
        m_prev = m_ref[head, rows]
        m_new = jnp.maximum(m_prev, jnp.max(col_max, axis=-1, keepdims=True))
        m_ref[head, rows] = m_new
        p = jnp.concatenate([jnp.exp2((column(c) - m_new).astype(BF16)) for c in range(n_col)],
                            axis=1)
        return head, blk, rows, jnp.exp2(m_prev - m_new), p

    def value_stage(head, blk, rows, alpha, p):
        start = pl.multiple_of(blk * tk, tk)
        v_ones = jnp.concatenate([v_ref[pl.ds(start, tk), v_lanes(head)], ones], axis=1)
        acc_ref[head, rows] = (jnp.concatenate([alpha, alpha], axis=1) * acc_ref[head, rows]
                               + jnp.dot(p, v_ones, preferred_element_type=F32))

    def fold(items):
        for item in items:
            head, _, slot, rows, _, _, next_blk = item
            if next_blk is not None:
                scores(head, next_blk, 1 - slot, rows)
            value_stage(*softmax_stage(item))

    for head in heads:
        scores(head, 0, 0, both)

    def pair(n, _):
        blk = 2 * n
        fold(tiles(blk, 0, both, False, next_blk=blk + 1)
             + tiles(blk + 1, 1, both, False, next_blk=blk + 2))
        return 0

    lax.fori_loop(0, i, pair, 0)

    for head in heads:
        scores(head, 2 * i + 1, 1, bottom)
    fold(tiles(2 * i, 0, top, True) + tiles(2 * i, 0, bottom, False)
         + tiles(2 * i + 1, 1, bottom, True))

    for head in heads:
        out = acc_ref[head, :, 0:MLA_V_DIM] * (1.0 / acc_ref[head, :, MLA_V_DIM:])
        gate = g_ref[:, v_lanes(head)].astype(F32)
        o_ref[:, v_lanes(head)] = (out * _silu(gate)).astype(o_ref.dtype)


def _mla_attn(q, k, v, main, *, tk, heads_per_step):
    s = q.shape[0]
    tq = 2 * tk
    n_groups = MLA_HEADS // heads_per_step
    qk_width = heads_per_step * MLA_QK_PAD
    v_width = heads_per_step * MLA_V_DIM
    gate_block = MLA_G_BLK // heads_per_step
    kern = functools.partial(_mla_attn_kernel, tk=tk)
    return pl.pallas_call(
        kern,
        grid=(n_groups, s // tq),
        in_specs=[
            pl.BlockSpec((tq, qk_width), lambda g, i: (i, g)),
            pl.BlockSpec((s, qk_width), lambda g, i: (0, g)),
            pl.BlockSpec((s, v_width), lambda g, i: (0, g)),
            pl.BlockSpec((tq, v_width), lambda g, i: (i, gate_block + g)),
        ],
        out_specs=pl.BlockSpec((tq, v_width), lambda g, i: (i, g)),
        out_shape=jax.ShapeDtypeStruct((s, MLA_WIDTH), BF16),
        scratch_shapes=[pltpu.VMEM((heads_per_step, 2, tq, tk), F32),
                        pltpu.VMEM((heads_per_step, tq, LANES), F32),
                        pltpu.VMEM((heads_per_step, tq, MLA_V_DIM + LANES), F32)],
        compiler_params=pltpu.CompilerParams(
            dimension_semantics=("arbitrary", "arbitrary"), vmem_limit_bytes=VMEM_LIMIT_BYTES),
        name="mla_attn",
    )(q, k, v, main)


def _out_proj_kernel(a_ref, b_ref, w_ref, x_ref, nw_ref, o_ref, wb_ref):
    @pl.when(pl.program_id(0) == 0)
    def _():
        wb_ref[...] = w_ref[...].astype(BF16)

    half = a_ref.shape[0] // 2
    for rows in (slice(0, half), slice(half, 2 * half)):
        y = (jnp.dot(a_ref[rows], wb_ref[0:SB_WIDTH, :], preferred_element_type=F32)
             + jnp.dot(b_ref[rows], wb_ref[SB_WIDTH:, :], preferred_element_type=F32))
        o_ref[rows] = x_ref[rows] + _rms_norm(y, nw_ref[...])


def _out_proj(mix_a, mix_b, w_out, x, norm_w, *, tm):
    s, d = x.shape
    row = lambda i: (i, 0)
    fixed = lambda i: (0, 0)
    return pl.pallas_call(
        _out_proj_kernel,
        grid=(s // tm,),
        in_specs=[
            pl.BlockSpec((tm, SB_WIDTH), row),
            pl.BlockSpec((tm, MLA_WIDTH), row),
            pl.BlockSpec(w_out.shape, fixed, pipeline_mode=pl.Buffered(1)),
            pl.BlockSpec((tm, d), row),
            pl.BlockSpec((1, d), fixed),
        ],
        out_specs=pl.BlockSpec((tm, d), row),
        out_shape=jax.ShapeDtypeStruct((s, d), F32),
        scratch_shapes=[pltpu.VMEM(w_out.shape, BF16)],
        compiler_params=pltpu.CompilerParams(
            dimension_semantics=("arbitrary",), vmem_limit_bytes=VMEM_LIMIT_BYTES),
        name="out_proj",
    )(mix_a, mix_b, w_out, x, norm_w)


def _rotate_half_cols(w):
    w1, w2 = jnp.split(w, 2, axis=-1)
    return jnp.concatenate([-w2, w1], axis=-1)


def _prep_q_up_weight(w_q_up):
    r = w_q_up.shape[0]
    w = w_q_up.reshape(r, MLA_HEADS, MLA_QK_DIM)
    nope, rope = w[:, :, :MLA_NOPE_DIM], w[:, :, MLA_NOPE_DIM:]
    pad = jnp.zeros((r, MLA_HEADS, LANES - MLA_ROPE_DIM), w.dtype)
    wq = jnp.concatenate([nope, rope, pad], axis=2).reshape(r, MLA_HEADS * MLA_QK_PAD)
    wqr = jnp.concatenate([_rotate_half_cols(rope), pad], axis=2).reshape(r, MLA_HEADS * LANES)
    return wq.astype(BF16), wqr.astype(BF16)


def _prep_kv_up_weight(w_kv_up):
    r = w_kv_up.shape[0]
    w = w_kv_up.reshape(r, MLA_HEADS, MLA_NOPE_DIM + MLA_V_DIM)
    k_nope = w[:, :, :MLA_NOPE_DIM].reshape(r, MLA_HEADS * MLA_NOPE_DIM)
    v = w[:, :, MLA_NOPE_DIM:].reshape(r, MLA_WIDTH)
    return jnp.concatenate([k_nope, v], axis=1).astype(BF16)


def _layer(x, cos, sin, pre_norm_w, w_in, q_norm_w, w_q_up, kv_norm_w, w_kv_up, w_out, post_norm_w):
    main, tail = _in_proj(x, pre_norm_w.reshape(1, -1), w_in.T, tm=1024)
    wq, wqr = _prep_q_up_weight(w_q_up)
    q_mla, k_mla, v_mla = _mla_prep(tail, cos, sin, q_norm_w.reshape(1, -1), kv_norm_w.reshape(1, -1),
                                    wq, wqr, _prep_kv_up_weight(w_kv_up), tm=1024)
    mix_a = _sb_attn(main, tq=256, tk=256, heads_per_step=8)
    mix_b = _mla_attn(q_mla, k_mla, v_mla, main, tk=512, heads_per_step=2)
    return _out_proj(mix_a, mix_b, w_out, x, post_norm_w.reshape(1, -1), tm=512)


def kernel(x, positions, pre_norm_w, w_in, q_norm_w, w_q_up, kv_norm_w, w_kv_up, w_out, post_norm_w):
    batch, depth = x.shape[0], pre_norm_w.shape[0]
    outs = []
    for b in range(batch):
        xb = x[b]
        cos, sin = _rope_tables(positions[b])
        for i in range(depth):
            xb = _layer(xb, cos, sin, pre_norm_w[i], w_in[i], q_norm_w[i], w_q_up[i],
                        kv_norm_w[i], w_kv_up[i], w_out[i], post_norm_w[i])
        outs.append(xb)
    return outs[0][None] if batch == 1 else jnp.stack(outs, axis=0)
```

```python
import functools
import math

import numpy as np
import jax
import jax.numpy as jnp
from jax import lax
from jax.experimental import pallas as pl
from jax.experimental.pallas import tpu as pltpu

F32 = jnp.float32
BF16 = jnp.bfloat16

LANES = 128
VMEM_LIMIT_BYTES = 56 * 1024 * 1024

CHUNK = 64
EPS = 1e-6
SB_HEADS = 8
SB_HEAD_DIM = 128
SB_WIDTH = SB_HEADS * SB_HEAD_DIM
MLA_HEADS = 8
MLA_NOPE_DIM = 128
MLA_ROPE_DIM = 64
MLA_QK_DIM = MLA_NOPE_DIM + MLA_ROPE_DIM
MLA_V_DIM = 128
MLA_WIDTH = MLA_HEADS * MLA_V_DIM
Q_LORA_RANK = 512
KV_LORA_RANK = 256
ROPE_THETA = 10000.0
HALF_ROPE = MLA_ROPE_DIM // 2
MLA_QK_PAD = 2 * LANES
LOG2_E = math.log2(math.e)
SB_SKIP_LOG2 = -150.0

MAIN_WIDTH = 4 * SB_WIDTH + MLA_WIDTH
TAIL_WIDTH = Q_LORA_RANK + KV_LORA_RANK + 2 * LANES
MLA_G_BLK = 4 * SB_WIDTH // LANES


def _rms_norm(x, w):
    return x * lax.rsqrt(jnp.mean(x * x, axis=-1, keepdims=True) + EPS) * w


def _silu(g):
    return g * (1.0 / (1.0 + jnp.exp(-g)))


def _dot_nt(a, b):
    return lax.dot_general(a, b, (((1,), (1,)), ((), ())), preferred_element_type=F32)


IN_TILE = 1024
N_SB = 4 * SB_WIDTH // IN_TILE
N_MAIN = MAIN_WIDTH // IN_TILE
N_IN_TILES = N_MAIN + TAIL_WIDTH // IN_TILE
C_Q_ROW = 4 * SB_WIDTH
MLA_GATE_ROW = C_Q_ROW + Q_LORA_RANK + KV_LORA_RANK + MLA_ROPE_DIM
assert SB_WIDTH % IN_TILE == 0 and MLA_WIDTH % IN_TILE == 0 and TAIL_WIDTH % IN_TILE == 0


def _in_proj_kernel(x_ref, nw_ref, wt_ref, main_ref, tail_ref, h_ref, *, q_scale):
    j = pl.program_id(1)

    @pl.when(j == 0)
    def _():
        h_ref[...] = _rms_norm(x_ref[...], nw_ref[...]).astype(BF16)

    def proj():
        return _dot_nt(h_ref[...], wt_ref[...].astype(BF16))

    @pl.when(j < SB_WIDTH // IN_TILE)
    def _():
        main_ref[...] = (proj() * q_scale).astype(BF16)

    @pl.when((j >= SB_WIDTH // IN_TILE) & (j < N_MAIN))
    def _():
        main_ref[...] = proj().astype(BF16)

    @pl.when(j >= N_MAIN)
    def _():
        tail_ref[...] = proj()


def _in_proj(x, norm_w, w_in_t, *, tm):
    s, d = x.shape
    kern = functools.partial(_in_proj_kernel, q_scale=LOG2_E / math.sqrt(SB_HEAD_DIM))

    def weight_row(i, j):
        unit = MLA_ROPE_DIM
        step = IN_TILE // unit
        row = jnp.where(j < N_SB, j * step,
                        jnp.where(j < N_MAIN, MLA_GATE_ROW // unit + (j - N_SB) * step,
                                  C_Q_ROW // unit + (j - N_MAIN) * step))
        return (row * unit, 0)

    return pl.pallas_call(
        kern,
        grid=(s // tm, N_IN_TILES),
        in_specs=[
            pl.BlockSpec((tm, d), lambda i, j: (i, 0)),
            pl.BlockSpec((1, d), lambda i, j: (0, 0)),
            pl.BlockSpec((pl.Element(IN_TILE), pl.Element(d)), weight_row),
        ],
        out_specs=[
            pl.BlockSpec((tm, IN_TILE), lambda i, j: (i, jnp.minimum(j, N_MAIN - 1))),
            pl.BlockSpec((tm, IN_TILE), lambda i, j: (i, jnp.maximum(j - N_MAIN, 0))),
        ],
        out_shape=[jax.ShapeDtypeStruct((s, MAIN_WIDTH), BF16),
                   jax.ShapeDtypeStruct((s, TAIL_WIDTH), F32)],
        scratch_shapes=[pltpu.VMEM((tm, d), BF16)],
        compiler_params=pltpu.CompilerParams(
            dimension_semantics=("arbitrary", "arbitrary"), vmem_limit_bytes=VMEM_LIMIT_BYTES),
        name="in_proj",
    )(x, norm_w, w_in_t)


def _rope_table_kernel(pos_ref, freq_ref, cos_ref, sin_ref):
    ang = freq_ref[...] * pos_ref[...].astype(F32)
    zeros = jnp.zeros((LANES - MLA_ROPE_DIM, ang.shape[1]), F32)
    cos = jnp.cos(ang)
    sin = jnp.sin(ang)
    cos_ref[...] = jnp.concatenate([cos, cos, zeros], axis=0)
    sin_ref[...] = jnp.concatenate([sin, sin, zeros], axis=0)


def _rope_tables(positions):
    s = positions.shape[0]
    inv_freq = ROPE_THETA ** (-jnp.arange(0, MLA_ROPE_DIM, 2, dtype=F32) / MLA_ROPE_DIM)
    shape = jax.ShapeDtypeStruct((LANES, s), F32)
    return pl.pallas_call(_rope_table_kernel, out_shape=[shape, shape], name="rope_tab")(
        positions.reshape(1, s), inv_freq.reshape(HALF_ROPE, 1))


def _mla_prep_kernel(tail_ref, cos_ref, sin_ref, qnw_ref, kvnw_ref, wq_ref, wqr_ref, wkv_ref,
                     q_ref, k_ref, v_ref, *, q_scale):
    cos = cos_ref[...].T
    sin = sin_ref[...].T
    c_q = tail_ref[:, 0:Q_LORA_RANK]
    hq = _rms_norm(c_q, qnw_ref[...]).astype(BF16)
    zq = jnp.dot(hq, wq_ref[...], preferred_element_type=F32)
    zr = jnp.dot(hq, wqr_ref[...], preferred_element_type=F32)
    for h in range(MLA_HEADS):
        lo = h * MLA_QK_PAD
        q_ref[:, lo:lo + LANES] = (zq[:, lo:lo + LANES] * q_scale).astype(BF16)
        rot = zq[:, lo + LANES:lo + 2 * LANES] * cos + zr[:, h * LANES:(h + 1) * LANES] * sin
        q_ref[:, lo + LANES:lo + 2 * LANES] = (rot * q_scale).astype(BF16)

    c_kv = tail_ref[:, Q_LORA_RANK:Q_LORA_RANK + KV_LORA_RANK]
    hkv = _rms_norm(c_kv, kvnw_ref[...]).astype(BF16)
    zkv = jnp.dot(hkv, wkv_ref[...], preferred_element_type=F32)
    kr0 = Q_LORA_RANK + KV_LORA_RANK
    kr = tail_ref[:, kr0:kr0 + LANES]
    lane = lax.broadcasted_iota(jnp.int32, kr.shape, 1)
    rotated = jnp.where(lane < HALF_ROPE,
                        -pltpu.roll(kr, LANES - HALF_ROPE, axis=1),
                        pltpu.roll(kr, HALF_ROPE, axis=1))
    k_rot = (kr * cos + rotated * sin).astype(BF16)
    for h in range(MLA_HEADS):
        lo = h * MLA_QK_PAD
        k_ref[:, lo:lo + LANES] = zkv[:, h * LANES:(h + 1) * LANES].astype(BF16)
        k_ref[:, lo + LANES:lo + 2 * LANES] = k_rot
    v_ref[...] = zkv[:, MLA_HEADS * MLA_NOPE_DIM:].astype(BF16)


def _mla_prep(tail, cos, sin, q_norm_w, kv_norm_w, wq, wqr, wkv, *, tm):
    s = tail.shape[0]
    row = lambda i: (i, 0)
    fixed = lambda i: (0, 0)
    kern = functools.partial(_mla_prep_kernel, q_scale=LOG2_E / math.sqrt(MLA_QK_DIM))
    return pl.pallas_call(
        kern,
        grid=(s // tm,),
        in_specs=[
            pl.BlockSpec((tm, TAIL_WIDTH), row),
            pl.BlockSpec((LANES, tm), lambda i: (0, i)),
            pl.BlockSpec((LANES, tm), lambda i: (0, i)),
            pl.BlockSpec(q_norm_w.shape, fixed),
            pl.BlockSpec(kv_norm_w.shape, fixed),
            pl.BlockSpec(wq.shape, fixed),
            pl.BlockSpec(wqr.shape, fixed),
            pl.BlockSpec(wkv.shape, fixed),
        ],
        out_specs=[
            pl.BlockSpec((tm, MLA_HEADS * MLA_QK_PAD), row),
            pl.BlockSpec((tm, MLA_HEADS * MLA_QK_PAD), row),
            pl.BlockSpec((tm, MLA_WIDTH), row),
        ],
        out_shape=[jax.ShapeDtypeStruct((s, MLA_HEADS * MLA_QK_PAD), BF16),
                   jax.ShapeDtypeStruct((s, MLA_HEADS * MLA_QK_PAD), BF16),
                   jax.ShapeDtypeStruct((s, MLA_WIDTH), BF16)],
        compiler_params=pltpu.CompilerParams(
            dimension_semantics=("arbitrary",), vmem_limit_bytes=VMEM_LIMIT_BYTES),
        name="mla_prep",
    )(tail, cos, sin, q_norm_w, kv_norm_w, wq, wqr, wkv)


def _sb_attn_kernel(q_ref, k_ref, v_ref, g_ref, o_ref, tri_ref, acc_ref, carry_ref, *, tq, tk):
    i = pl.program_id(1)
    n_heads = q_ref.shape[1] // SB_HEAD_DIM

    rows = lax.broadcasted_iota(jnp.int32, (tk, tk), 0)
    cols = lax.broadcasted_iota(jnp.int32, (tk, tk), 1)
    tri = jnp.where(rows >= cols, -1.0, 0.0).astype(BF16)
    tri_ref[0:tk, :] = tri
    tri_ref[tk:2 * tk, :] = tri
    acc_ref[...] = jnp.zeros_like(acc_ref)
    carry_ref[...] = jnp.zeros_like(carry_ref)

    def block(start, masked):
        heads = range(n_heads)
        lanes = [slice(h * SB_HEAD_DIM, (h + 1) * SB_HEAD_DIM) for h in heads]
        z = [_dot_nt(q_ref[:, lanes[h]], k_ref[pl.ds(start, tk), lanes[h]]) for h in heads]
        if masked:
            t_idx = i * tq + lax.broadcasted_iota(jnp.int32, (tq, tk), 0)
            s_idx = start + lax.broadcasted_iota(jnp.int32, (tq, tk), 1)
            before = s_idx < t_idx
            z = [jnp.where(before, z[h], -jnp.inf) for h in heads]
        sp = [jnp.maximum(z[h], 0.0) + jnp.log2(1.0 + jnp.exp2(-jnp.abs(z[h]))) for h in heads]
        tri = tri_ref[...]
        sums = []
        for h in heads:
            hi = sp[h].astype(BF16)
            lo = (sp[h] - hi.astype(F32)).astype(BF16)
            sums.append(jnp.dot(jnp.concatenate([hi, lo], axis=1), tri,
                                preferred_element_type=F32))
        carry = [carry_ref[:, lanes[h]] for h in heads]
        a = [jnp.exp2(z[h] + sums[h] + jnp.concatenate([carry[h]] * (tk // LANES), axis=1))
             for h in heads]
        for h in heads:
            acc_ref[:, lanes[h]] += jnp.dot(a[h].astype(BF16), v_ref[pl.ds(start, tk), lanes[h]],
                                            preferred_element_type=F32)
            carry_ref[:, lanes[h]] = carry[h] + jnp.broadcast_to(sums[h][:, 0:1], (tq, LANES))

    n_diag = tq // tk
    for d in reversed(range(n_diag)):
        block(pl.multiple_of(i * tq + d * tk, tk), True)

    def live():
        return (jnp.max(carry_ref[...]) > SB_SKIP_LOG2).astype(jnp.int32)

    def cond(state):
        j, alive = state
        return (j >= 0) & (alive > 0)

    def body(state):
        j, _ = state
        block(pl.multiple_of(j * tk, tk), False)
        return j - 1, live()

    lax.while_loop(cond, body, (i * n_diag - 1, live()))
    o_ref[...] = (acc_ref[...] * _silu(g_ref[...].astype(F32))).astype(o_ref.dtype)


def _sb_attn(main, *, tq, tk, heads_per_step):
    s = main.shape[0]
    n_groups = SB_HEADS // heads_per_step
    width = heads_per_step * SB_HEAD_DIM
    kern = functools.partial(_sb_attn_kernel, tq=tq, tk=tk)
    resident = pl.Buffered(1) if n_groups == 1 else None
    return pl.pallas_call(
        kern,
        grid=(n_groups, s // tq),
        in_specs=[
            pl.BlockSpec((tq, width), lambda g, i: (i, g)),
            pl.BlockSpec((s, width), lambda g, i: (0, n_groups + g), pipeline_mode=resident),
            pl.BlockSpec((s, width), lambda g, i: (0, 2 * n_groups + g), pipeline_mode=resident),
            pl.BlockSpec((tq, width), lambda g, i: (i, 3 * n_groups + g)),
        ],
        out_specs=pl.BlockSpec((tq, width), lambda g, i: (i, g)),
        out_shape=jax.ShapeDtypeStruct((s, SB_WIDTH), BF16),
        scratch_shapes=[pltpu.VMEM((2 * tk, tk), BF16),
                        pltpu.VMEM((tq, width), F32),
                        pltpu.VMEM((tq, width), F32)],
        compiler_params=pltpu.CompilerParams(
            dimension_semantics=("arbitrary", "arbitrary"), vmem_limit_bytes=VMEM_LIMIT_BYTES),
        name="sb_attn",
    )(main, main, main, main)


def _mla_attn_kernel(q_ref, k_ref, v_ref, g_ref, o_ref, s_ref, m_ref, acc_ref, *, tk):
    i = pl.program_id(1)
    tq = 2 * tk
    n_col = tk // LANES
    n_heads = v_ref.shape[1] // MLA_V_DIM
    heads = range(n_heads)
    row_tile = 512
    top, bottom, both = slice(0, tk), slice(tk, tq), slice(0, tq)
    ones = jnp.ones((tk, LANES), BF16)
    m_ref[...] = jnp.full_like(m_ref, -jnp.inf)
    acc_ref[...] = jnp.zeros_like(acc_ref)

    def qk_lanes(head):
        return slice(head * MLA_QK_PAD, (head + 1) * MLA_QK_PAD)

    def v_lanes(head):
        return slice(head * MLA_V_DIM, (head + 1) * MLA_V_DIM)

    def scores(head, blk, slot, rows):
        start = pl.multiple_of(blk * tk, tk)
        s_ref[head, slot, rows] = _dot_nt(q_ref[rows, qk_lanes(head)],
                                          k_ref[pl.ds(start, tk), qk_lanes(head)])

    def tiles(blk, slot, rows, masked, next_blk=None):
        return [(head, blk, slot, slice(r0, r0 + row_tile), r0 - rows.start, masked, next_blk)
                for r0 in range(rows.start, rows.stop, row_tile) for head in heads]

    def softmax_stage(item):
        head, blk, slot, rows, row_offset, masked, _ = item

        def column(c):
            s = s_ref[head, slot, rows, c * LANES:(c + 1) * LANES]
            if masked:
                t_chunk = (row_offset + lax.broadcasted_iota(jnp.int32, (row_tile, LANES), 0)) // CHUNK
                s_chunk = (c * LANES + lax.broadcasted_iota(jnp.int32, (row_tile, LANES), 1)) // CHUNK
                s = jnp.where(s_chunk <= t_chunk, s, -jnp.inf)
            return s

        col_max = column(0)
        for c in range(1, n_col):
            col_max = jnp.maximum(col_max, column(c))
        m_prev = m_ref[head, rows]
        m_new = jnp.maximum(m_prev, jnp.max(col_max, axis=-1, keepdims=True))
        m_ref[head, rows] = m_new
        p = jnp.concatenate([jnp.exp2((column(c) - m_new).astype(BF16)) for c in range(n_col)],
                            axis=1)
        return head, blk, rows, jnp.exp2(m_prev - m_new), p

    def value_stage(head, blk, rows, alpha, p):
        start = pl.multiple_of(blk * tk, tk)
        v_ones = jnp.concatenate([v_ref[pl.ds(start, tk), v_lanes(head)], ones], axis=1)
        acc_ref[head, rows] = (jnp.concatenate([alpha, alpha], axis=1) * acc_ref[head, rows]
                               + jnp.dot(p, v_ones, preferred_element_type=F32))

    def fold(items):
        for item in items:
            head, _, slot, rows, _, _, next_blk = item
            if next_blk is not None:
                scores(head, next_blk, 1 - slot, rows)
            value_stage(*softmax_stage(item))

    for head in heads:
        scores(head, 0, 0, both)

    def pair(n, _):
        blk = 2 * n
        fold(tiles(blk, 0, both, False, next_blk=blk + 1)
             + tiles(blk + 1, 1, both, False, next_blk=blk + 2))
        return 0

    lax.fori_loop(0, i, pair, 0)

    for head in heads:
        scores(head, 2 * i + 1, 1, bottom)
    fold(tiles(2 * i, 0, top, True) + tiles(2 * i, 0, bottom, False)
         + tiles(2 * i + 1, 1, bottom, True))

    for head in heads:
        out = acc_ref[head, :, 0:MLA_V_DIM] * (1.0 / acc_ref[head, :, MLA_V_DIM:])
        gate = g_ref[:, v_lanes(head)].astype(F32)
        o_ref[:, v_lanes(head)] = (out * _silu(gate)).astype(o_ref.dtype)


def _mla_attn(q, k, v, main, *, tk, heads_per_step):
    s = q.shape[0]
    tq = 2 * tk
    n_groups = MLA_HEADS // heads_per_step
    qk_width = heads_per_step * MLA_QK_PAD
    v_width = heads_per_step * MLA_V_DIM
    gate_block = MLA_G_BLK // heads_per_step
    kern = functools.partial(_mla_attn_kernel, tk=tk)
    return pl.pallas_call(
        kern,
        grid=(n_groups, s // tq),
        in_specs=[
            pl.BlockSpec((tq, qk_width), lambda g, i: (i, g)),
            pl.BlockSpec((s, qk_width), lambda g, i: (0, g)),
            pl.BlockSpec((s, v_width), lambda g, i: (0, g)),
            pl.BlockSpec((tq, v_width), lambda g, i: (i, gate_block + g)),
        ],
        out_specs=pl.BlockSpec((tq, v_width), lambda g, i: (i, g)),
        out_shape=jax.ShapeDtypeStruct((s, MLA_WIDTH), BF16),
        scratch_shapes=[pltpu.VMEM((heads_per_step, 2, tq, tk), F32),
                        pltpu.VMEM((heads_per_step, tq, LANES), F32),
                        pltpu.VMEM((heads_per_step, tq, MLA_V_DIM + LANES), F32)],
        compiler_params=pltpu.CompilerParams(
            dimension_semantics=("arbitrary", "arbitrary"), vmem_limit_bytes=VMEM_LIMIT_BYTES),
        name="mla_attn",
    )(q, k, v, main)


def _out_proj_kernel(a_ref, b_ref, w_ref, x_ref, nw_ref, o_ref, wb_ref):
    @pl.when(pl.program_id(0) == 0)
    def _():
        wb_ref[...] = w_ref[...].astype(BF16)

    y = (jnp.dot(a_ref[...], wb_ref[0:SB_WIDTH, :], preferred_element_type=F32)
         + jnp.dot(b_ref[...], wb_ref[SB_WIDTH:, :], preferred_element_type=F32))
    o_ref[...] = x_ref[...] + _rms_norm(y, nw_ref[...])


def _out_proj(mix_a, mix_b, w_out, x, norm_w, *, tm):
    s, d = x.shape
    row = lambda i: (i, 0)
    fixed = lambda i: (0, 0)
    return pl.pallas_call(
        _out_proj_kernel,
        grid=(s // tm,),
        in_specs=[
            pl.BlockSpec((tm, SB_WIDTH), row),
            pl.BlockSpec((tm, MLA_WIDTH), row),
            pl.BlockSpec(w_out.shape, fixed, pipeline_mode=pl.Buffered(1)),
            pl.BlockSpec((tm, d), row),
            pl.BlockSpec((1, d), fixed),
        ],
        out_specs=pl.BlockSpec((tm, d), row),
        out_shape=jax.ShapeDtypeStruct((s, d), F32),
        scratch_shapes=[pltpu.VMEM(w_out.shape, BF16)],
        compiler_params=pltpu.CompilerParams(
            dimension_semantics=("arbitrary",), vmem_limit_bytes=VMEM_LIMIT_BYTES),
        name="out_proj",
    )(mix_a, mix_b, w_out, x, norm_w)


def _rotate_half_cols(w):
    w1, w2 = jnp.split(w, 2, axis=-1)
    return jnp.concatenate([-w2, w1], axis=-1)


def _prep_q_up_weight(w_q_up):
    r = w_q_up.shape[0]
    w = w_q_up.reshape(r, MLA_HEADS, MLA_QK_DIM)
    nope, rope = w[:, :, :MLA_NOPE_DIM], w[:, :, MLA_NOPE_DIM:]
    pad = jnp.zeros((r, MLA_HEADS, LANES - MLA_ROPE_DIM), w.dtype)
    wq = jnp.concatenate([nope, rope, pad], axis=2).reshape(r, MLA_HEADS * MLA_QK_PAD)
    wqr = jnp.concatenate([_rotate_half_cols(rope), pad], axis=2).reshape(r, MLA_HEADS * LANES)
    return wq.astype(BF16), wqr.astype(BF16)


def _prep_kv_up_weight(w_kv_up):
    r = w_kv_up.shape[0]
    w = w_kv_up.reshape(r, MLA_HEADS, MLA_NOPE_DIM + MLA_V_DIM)
    k_nope = w[:, :, :MLA_NOPE_DIM].reshape(r, MLA_HEADS * MLA_NOPE_DIM)
    v = w[:, :, MLA_NOPE_DIM:].reshape(r, MLA_WIDTH)
    return jnp.concatenate([k_nope, v], axis=1).astype(BF16)


def _layer(x, cos, sin, pre_norm_w, w_in, q_norm_w, w_q_up, kv_norm_w, w_kv_up, w_out, post_norm_w):
    main, tail = _in_proj(x, pre_norm_w.reshape(1, -1), w_in.T, tm=1024)
    wq, wqr = _prep_q_up_weight(w_q_up)
    q_mla, k_mla, v_mla = _mla_prep(tail, cos, sin, q_norm_w.reshape(1, -1), kv_norm_w.reshape(1, -1),
                                    wq, wqr, _prep_kv_up_weight(w_kv_up), tm=1024)
    mix_a = _sb_attn(main, tq=256, tk=256, heads_per_step=8)
    mix_b = _mla_attn(q_mla, k_mla, v_mla, main, tk=512, heads_per_step=2)
    return _out_proj(mix_a, mix_b, w_out, x, post_norm_w.reshape(1, -1), tm=512)


def kernel(x, positions, pre_norm_w, w_in, q_norm_w, w_q_up, kv_norm_w, w_kv_up, w_out, post_norm_w):
    batch, depth = x.shape[0], pre_norm_w.shape[0]
    outs = []
    for b in range(batch):
        xb = x[b]
        cos, sin = _rope_tables(positions[b])
        for i in range(depth):
            xb = _layer(xb, cos, sin, pre_norm_w[i], w_in[i], q_norm_w[i], w_q_up[i],
                        kv_norm_w[i], w_kv_up[i], w_out[i], post_norm_w[i])
        outs.append(xb)
    return outs[0][None] if batch == 1 else jnp.stack(outs, axis=0)
```

```python
import functools
import math

import numpy as np
import jax
import jax.numpy as jnp
from jax import lax
from jax.experimental import pallas as pl
from jax.experimental.pallas import tpu as pltpu

F32 = jnp.float32
BF16 = jnp.bfloat16

LANES = 128
VMEM_LIMIT_BYTES = 56 * 1024 * 1024

CHUNK = 64
EPS = 1e-6
SB_HEADS = 8
SB_HEAD_DIM = 128
SB_WIDTH = SB_HEADS * SB_HEAD_DIM
MLA_HEADS = 8
MLA_NOPE_DIM = 128
MLA_ROPE_DIM = 64
MLA_QK_DIM = MLA_NOPE_DIM + MLA_ROPE_DIM
MLA_V_DIM = 128
MLA_WIDTH = MLA_HEADS * MLA_V_DIM
Q_LORA_RANK = 512
KV_LORA_RANK = 256
ROPE_THETA = 10000.0
HALF_ROPE = MLA_ROPE_DIM // 2
MLA_QK_PAD = 2 * LANES
LOG2_E = math.log2(math.e)
SB_SKIP_LOG2 = -150.0

MAIN_WIDTH = 4 * SB_WIDTH + MLA_WIDTH
TAIL_WIDTH = Q_LORA_RANK + KV_LORA_RANK + 2 * LANES
MLA_G_BLK = 4 * SB_WIDTH // LANES


def _rms_norm(x, w):
    return x * lax.rsqrt(jnp.mean(x * x, axis=-1, keepdims=True) + EPS) * w


def _silu(g):
    return g * (1.0 / (1.0 + jnp.exp(-g)))


def _dot_nt(a, b):
    return lax.dot_general(a, b, (((1,), (1,)), ((), ())), preferred_element_type=F32)


IN_TILE = 1024
N_SB = 4 * SB_WIDTH // IN_TILE
N_MAIN = MAIN_WIDTH // IN_TILE
N_IN_TILES = N_MAIN + TAIL_WIDTH // IN_TILE
C_Q_ROW = 4 * SB_WIDTH
MLA_GATE_ROW = C_Q_ROW + Q_LORA_RANK + KV_LORA_RANK + MLA_ROPE_DIM
assert SB_WIDTH % IN_TILE == 0 and MLA_WIDTH % IN_TILE == 0 and TAIL_WIDTH % IN_TILE == 0


def _in_proj_kernel(x_ref, nw_ref, wt_ref, main_ref, tail_ref, h_ref, *, q_scale):
    j = pl.program_id(1)

    @pl.when(j == 0)
    def _():
        h_ref[...] = _rms_norm(x_ref[...], nw_ref[...]).astype(BF16)

    def proj():
        return _dot_nt(h_ref[...], wt_ref[...].astype(BF16))

    @pl.when(j < SB_WIDTH // IN_TILE)
    def _():
        main_ref[...] = (proj() * q_scale).astype(BF16)

    @pl.when((j >= SB_WIDTH // IN_TILE) & (j < N_MAIN))
    def _():
        main_ref[...] = proj().astype(BF16)

    @pl.when(j >= N_MAIN)
    def _():
        tail_ref[...] = proj()


def _in_proj(x, norm_w, w_in_t, *, tm):
    s, d = x.shape
    kern = functools.partial(_in_proj_kernel, q_scale=LOG2_E / math.sqrt(SB_HEAD_DIM))

    def weight_row(i, j):
        unit = MLA_ROPE_DIM
        step = IN_TILE // unit
        row = jnp.where(j < N_SB, j * step,
                        jnp.where(j < N_MAIN, MLA_GATE_ROW // unit + (j - N_SB) * step,
                                  C_Q_ROW // unit + (j - N_MAIN) * step))
        return (row * unit, 0)

    return pl.pallas_call(
        kern,
        grid=(s // tm, N_IN_TILES),
        in_specs=[
            pl.BlockSpec((tm, d), lambda i, j: (i, 0)),
            pl.BlockSpec((1, d), lambda i, j: (0, 0)),
            pl.BlockSpec((pl.Element(IN_TILE), pl.Element(d)), weight_row),
        ],
        out_specs=[
            pl.BlockSpec((tm, IN_TILE), lambda i, j: (i, jnp.minimum(j, N_MAIN - 1))),
            pl.BlockSpec((tm, IN_TILE), lambda i, j: (i, jnp.maximum(j - N_MAIN, 0))),
        ],
        out_shape=[jax.ShapeDtypeStruct((s, MAIN_WIDTH), BF16),
                   jax.ShapeDtypeStruct((s, TAIL_WIDTH), F32)],
        scratch_shapes=[pltpu.VMEM((tm, d), BF16)],
        compiler_params=pltpu.CompilerParams(
            dimension_semantics=("arbitrary", "arbitrary"), vmem_limit_bytes=VMEM_LIMIT_BYTES),
        name="in_proj",
    )(x, norm_w, w_in_t)


def _rope_table_kernel(pos_ref, freq_ref, cos_ref, sin_ref):
    ang = freq_ref[...] * pos_ref[...].astype(F32)
    cos = jnp.cos(ang)
    sin = jnp.sin(ang)
    cos_ref[...] = jnp.concatenate([cos] * (LANES // HALF_ROPE), axis=0)
    sin_ref[...] = jnp.concatenate([sin] * (LANES // HALF_ROPE), axis=0)


def _rope_tables(positions):
    s = positions.shape[0]
    inv_freq = ROPE_THETA ** (-jnp.arange(0, MLA_ROPE_DIM, 2, dtype=F32) / MLA_ROPE_DIM)
    shape = jax.ShapeDtypeStruct((LANES, s), F32)
    return pl.pallas_call(_rope_table_kernel, out_shape=[shape, shape], name="rope_tab")(
        positions.reshape(1, s), inv_freq.reshape(HALF_ROPE, 1))


def _mla_prep_kernel(tail_ref, cos_ref, sin_ref, qnw_ref, kvnw_ref, wq_ref, wkv_ref,
                     q_ref, k_ref, v_ref, *, q_scale):
    cos = cos_ref[...].T
    sin = sin_ref[...].T
    lane = lax.broadcasted_iota(jnp.int32, cos.shape, 1)
    first_half = lane < MLA_ROPE_DIM
    c_q = tail_ref[:, 0:Q_LORA_RANK]
    hq = _rms_norm(c_q, qnw_ref[...]).astype(BF16)
    zq = jnp.dot(hq, wq_ref[...], preferred_element_type=F32)
    rope0 = MLA_HEADS * MLA_NOPE_DIM
    rot0 = rope0 + MLA_HEADS * MLA_ROPE_DIM
    for pair in range(MLA_HEADS // 2):
        lanes = slice(pair * LANES, (pair + 1) * LANES)
        roped = (zq[:, rope0:rot0][:, lanes] * cos + zq[:, rot0:][:, lanes] * sin)
        for h, mine in ((2 * pair, roped), (2 * pair + 1, pltpu.roll(roped, MLA_ROPE_DIM, axis=1))):
            lo = h * MLA_QK_PAD
            q_ref[:, lo:lo + LANES] = (zq[:, h * LANES:(h + 1) * LANES] * q_scale).astype(BF16)
            q_ref[:, lo + LANES:lo + 2 * LANES] = (
                jnp.where(first_half, mine, 0.0) * q_scale).astype(BF16)

    c_kv = tail_ref[:, Q_LORA_RANK:Q_LORA_RANK + KV_LORA_RANK]
    hkv = _rms_norm(c_kv, kvnw_ref[...]).astype(BF16)
    zkv = jnp.dot(hkv, wkv_ref[...], preferred_element_type=F32)
    kr0 = Q_LORA_RANK + KV_LORA_RANK
    kr = tail_ref[:, kr0:kr0 + LANES]
    rotated = jnp.where(lane < HALF_ROPE,
                        -pltpu.roll(kr, LANES - HALF_ROPE, axis=1),
                        pltpu.roll(kr, HALF_ROPE, axis=1))
    k_rot = jnp.where(first_half, kr * cos + rotated * sin, 0.0).astype(BF16)
    for h in range(MLA_HEADS):
        lo = h * MLA_QK_PAD
        k_ref[:, lo:lo + LANES] = zkv[:, h * LANES:(h + 1) * LANES].astype(BF16)
        k_ref[:, lo + LANES:lo + 2 * LANES] = k_rot
    v_ref[...] = zkv[:, MLA_HEADS * MLA_NOPE_DIM:].astype(BF16)


def _mla_prep(tail, cos, sin, q_norm_w, kv_norm_w, wq, wkv, *, tm):
    s = tail.shape[0]
    row = lambda i: (i, 0)
    fixed = lambda i: (0, 0)
    kern = functools.partial(_mla_prep_kernel, q_scale=LOG2_E / math.sqrt(MLA_QK_DIM))
    return pl.pallas_call(
        kern,
        grid=(s // tm,),
        in_specs=[
            pl.BlockSpec((tm, TAIL_WIDTH), row),
            pl.BlockSpec((LANES, tm), lambda i: (0, i)),
            pl.BlockSpec((LANES, tm), lambda i: (0, i)),
            pl.BlockSpec(q_norm_w.shape, fixed),
            pl.BlockSpec(kv_norm_w.shape, fixed),
            pl.BlockSpec(wq.shape, fixed),
            pl.BlockSpec(wkv.shape, fixed),
        ],
        out_specs=[
            pl.BlockSpec((tm, MLA_HEADS * MLA_QK_PAD), row),
            pl.BlockSpec((tm, MLA_HEADS * MLA_QK_PAD), row),
            pl.BlockSpec((tm, MLA_WIDTH), row),
        ],
        out_shape=[jax.ShapeDtypeStruct((s, MLA_HEADS * MLA_QK_PAD), BF16),
                   jax.ShapeDtypeStruct((s, MLA_HEADS * MLA_QK_PAD), BF16),
                   jax.ShapeDtypeStruct((s, MLA_WIDTH), BF16)],
        compiler_params=pltpu.CompilerParams(
            dimension_semantics=("arbitrary",), vmem_limit_bytes=VMEM_LIMIT_BYTES),
        name="mla_prep",
    )(tail, cos, sin, q_norm_w, kv_norm_w, wq, wkv)


def _sb_attn_kernel(q_ref, k_ref, v_ref, g_ref, o_ref, tri_ref, acc_ref, carry_ref, *, tq, tk):
    i = pl.program_id(1)
    n_heads = q_ref.shape[1] // SB_HEAD_DIM

    rows = lax.broadcasted_iota(jnp.int32, (tk, tk), 0)
    cols = lax.broadcasted_iota(jnp.int32, (tk, tk), 1)
    tri = jnp.where(rows >= cols, -1.0, 0.0).astype(BF16)
    tri_ref[0:tk, :] = tri
    tri_ref[tk:2 * tk, :] = tri
    acc_ref[...] = jnp.zeros_like(acc_ref)
    carry_ref[...] = jnp.zeros_like(carry_ref)

    def block(start, masked):
        heads = range(n_heads)
        lanes = [slice(h * SB_HEAD_DIM, (h + 1) * SB_HEAD_DIM) for h in heads]
        z = [_dot_nt(q_ref[:, lanes[h]], k_ref[pl.ds(start, tk), lanes[h]]) for h in heads]
        if masked:
            t_idx = i * tq + lax.broadcasted_iota(jnp.int32, (tq, tk), 0)
            s_idx = start + lax.broadcasted_iota(jnp.int32, (tq, tk), 1)
            before = s_idx < t_idx
            z = [jnp.where(before, z[h], -jnp.inf) for h in heads]
        sp = [jnp.maximum(z[h], 0.0) + jnp.log2(1.0 + jnp.exp2(-jnp.abs(z[h]))) for h in heads]
        tri = tri_ref[...]
        sums = []
        for h in heads:
            hi = sp[h].astype(BF16)
            lo = (sp[h] - hi.astype(F32)).astype(BF16)
            sums.append(jnp.dot(jnp.concatenate([hi, lo], axis=1), tri,
                                preferred_element_type=F32))
        carry = [carry_ref[:, lanes[h]] for h in heads]
        a = [jnp.exp2(z[h] + sums[h] + jnp.concatenate([carry[h]] * (tk // LANES), axis=1))
             for h in heads]
        for h in heads:
            acc_ref[:, lanes[h]] += jnp.dot(a[h].astype(BF16), v_ref[pl.ds(start, tk), lanes[h]],
                                            preferred_element_type=F32)
            carry_ref[:, lanes[h]] = carry[h] + jnp.broadcast_to(sums[h][:, 0:1], (tq, LANES))

    n_diag = tq // tk
    for d in reversed(range(n_diag)):
        block(pl.multiple_of(i * tq + d * tk, tk), True)

    def live():
        return (jnp.max(carry_ref[...]) > SB_SKIP_LOG2).astype(jnp.int32)

    def cond(state):
        j, alive = state
        return (j >= 0) & (alive > 0)

    def body(state):
        j, _ = state
        block(pl.multiple_of(j * tk, tk), False)
        return j - 1, live()

    lax.while_loop(cond, body, (i * n_diag - 1, live()))
    o_ref[...] = (acc_ref[...] * _silu(g_ref[...].astype(F32))).astype(o_ref.dtype)


def _sb_attn(main, *, tq, tk, heads_per_step):
    s = main.shape[0]
    n_groups = SB_HEADS // heads_per_step
    width = heads_per_step * SB_HEAD_DIM
    kern = functools.partial(_sb_attn_kernel, tq=tq, tk=tk)
    resident = pl.Buffered(1) if n_groups == 1 else None
    return pl.pallas_call(
        kern,
        grid=(n_groups, s // tq),
        in_specs=[
            pl.BlockSpec((tq, width), lambda g, i: (i, g)),
            pl.BlockSpec((s, width), lambda g, i: (0, n_groups + g), pipeline_mode=resident),
            pl.BlockSpec((s, width), lambda g, i: (0, 2 * n_groups + g), pipeline_mode=resident),
            pl.BlockSpec((tq, width), lambda g, i: (i, 3 * n_groups + g)),
        ],
        out_specs=pl.BlockSpec((tq, width), lambda g, i: (i, g)),
        out_shape=jax.ShapeDtypeStruct((s, SB_WIDTH), BF16),
        scratch_shapes=[pltpu.VMEM((2 * tk, tk), BF16),
                        pltpu.VMEM((tq, width), F32),
                        pltpu.VMEM((tq, width), F32)],
        compiler_params=pltpu.CompilerParams(
            dimension_semantics=("arbitrary", "arbitrary"), vmem_limit_bytes=VMEM_LIMIT_BYTES),
        name="sb_attn",
    )(main, main, main, main)


def _mla_attn_kernel(q_ref, k_ref, v_ref, g_ref, o_ref, s_ref, m_ref, acc_ref, *, tk):
    i = pl.program_id(1)
    tq = 2 * tk
    n_col = tk // LANES
    n_heads = v_ref.shape[1] // MLA_V_DIM
    heads = range(n_heads)
    row_tile = 512
    top, bottom, both = slice(0, tk), slice(tk, tq), slice(0, tq)
    ones = jnp.ones((tk, LANES), BF16)
    m_ref[...] = jnp.full_like(m_ref, -jnp.inf)
    acc_ref[...] = jnp.zeros_like(acc_ref)

    def qk_lanes(head):
        return slice(head * MLA_QK_PAD, (head + 1) * MLA_QK_PAD)

    def v_lanes(head):
        return slice(head * MLA_V_DIM, (head + 1) * MLA_V_DIM)

    def scores(head, blk, slot, rows):
        start = pl.multiple_of(blk * tk, tk)
        s_ref[head, slot, rows] = _dot_nt(q_ref[rows, qk_lanes(head)],
                                          k_ref[pl.ds(start, tk), qk_lanes(head)])

    def tiles(blk, slot, rows, masked, next_blk=None):
        return [(head, blk, slot, slice(r0, r0 + row_tile), r0 - rows.start, masked, next_blk)
                for r0 in range(rows.start, rows.stop, row_tile) for head in heads]

    def softmax_stage(item):
        head, blk, slot, rows, row_offset, masked, _ = item

        def column(c):
            s = s_ref[head, slot, rows, c * LANES:(c + 1) * LANES]
            if masked:
                t_chunk = (row_offset + lax.broadcasted_iota(jnp.int32, (row_tile, LANES), 0)) // CHUNK
                s_chunk = (c * LANES + lax.broadcasted_iota(jnp.int32, (row_tile, LANES), 1)) // CHUNK
                s = jnp.where(s_chunk <= t_chunk, s, -jnp.inf)
            return s

        col_max = column(0)
        for c in range(1, n_col):
            col_max = jnp.maximum(col_max, column(c))
        m_prev = m_ref[head, rows]
        m_new = jnp.maximum(m_prev, jnp.max(col_max, axis=-1, keepdims=True))
        m_ref[head, rows] = m_new
        p = jnp.concatenate([jnp.exp2((column(c) - m_new).astype(BF16)) for c in range(n_col)],
                            axis=1)
        return head, blk, rows, jnp.exp2(m_prev - m_new), p

    def value_stage(head, blk, rows, alpha, p):
        start = pl.multiple_of(blk * tk, tk)
        v_ones = jnp.concatenate([v_ref[pl.ds(start, tk), v_lanes(head)], ones], axis=1)
        acc_ref[head, rows] = (jnp.concatenate([alpha, alpha], axis=1) * acc_ref[head, rows]
                               + jnp.dot(p, v_ones, preferred_element_type=F32))

    def fold(items):
        for item in items:
            head, _, slot, rows, _, _, next_blk = item
            if next_blk is not None:
                scores(head, next_blk, 1 - slot, rows)
            value_stage(*softmax_stage(item))

    for head in heads:
        scores(head, 0, 0, both)

    def pair(n, _):
        blk = 2 * n
        fold(tiles(blk, 0, both, False, next_blk=blk + 1)
             + tiles(blk + 1, 1, both, False, next_blk=blk + 2))
        return 0

    lax.fori_loop(0, i, pair, 0)

    for head in heads:
        scores(head, 2 * i + 1, 1, bottom)
    fold(tiles(2 * i, 0, top, True) + tiles(2 * i, 0, bottom, False)
         + tiles(2 * i + 1, 1, bottom, True))

    for head in heads:
        out = acc_ref[head, :, 0:MLA_V_DIM] * (1.0 / acc_ref[head, :, MLA_V_DIM:])
        gate = g_ref[:, v_lanes(head)].astype(F32)
        o_ref[:, v_lanes(head)] = (out * _silu(gate)).astype(o_ref.dtype)


def _mla_attn(q, k, v, main, *, tk, heads_per_step):
    s = q.shape[0]
    tq = 2 * tk
    n_groups = MLA_HEADS // heads_per_step
    qk_width = heads_per_step * MLA_QK_PAD
    v_width = heads_per_step * MLA_V_DIM
    gate_block = MLA_G_BLK // heads_per_step
    kern = functools.partial(_mla_attn_kernel, tk=tk)
    return pl.pallas_call(
        kern,
        grid=(n_groups, s // tq),
        in_specs=[
            pl.BlockSpec((tq, qk_width), lambda g, i: (i, g)),
            pl.BlockSpec((s, qk_width), lambda g, i: (0, g)),
            pl.BlockSpec((s, v_width), lambda g, i: (0, g)),
            pl.BlockSpec((tq, v_width), lambda g, i: (i, gate_block + g)),
        ],
        out_specs=pl.BlockSpec((tq, v_width), lambda g, i: (i, g)),
        out_shape=jax.ShapeDtypeStruct((s, MLA_WIDTH), BF16),
        scratch_shapes=[pltpu.VMEM((heads_per_step, 2, tq, tk), F32),
                        pltpu.VMEM((heads_per_step, tq, LANES), F32),
                        pltpu.VMEM((heads_per_step, tq, MLA_V_DIM + LANES), F32)],
        compiler_params=pltpu.CompilerParams(
            dimension_semantics=("arbitrary", "arbitrary"), vmem_limit_bytes=VMEM_LIMIT_BYTES),
        name="mla_attn",
    )(q, k, v, main)


def _out_proj_kernel(a_ref, b_ref, w_ref, x_ref, nw_ref, o_ref, wb_ref):
    @pl.when(pl.program_id(0) == 0)
    def _():
        wb_ref[...] = w_ref[...].astype(BF16)

    y = (jnp.dot(a_ref[...], wb_ref[0:SB_WIDTH, :], preferred_element_type=F32)
         + jnp.dot(b_ref[...], wb_ref[SB_WIDTH:, :], preferred_element_type=F32))
    o_ref[...] = x_ref[...] + _rms_norm(y, nw_ref[...])


def _out_proj(mix_a, mix_b, w_out, x, norm_w, *, tm):
    s, d = x.shape
    row = lambda i: (i, 0)
    fixed = lambda i: (0, 0)
    return pl.pallas_call(
        _out_proj_kernel,
        grid=(s // tm,),
        in_specs=[
            pl.BlockSpec((tm, SB_WIDTH), row),
            pl.BlockSpec((tm, MLA_WIDTH), row),
            pl.BlockSpec(w_out.shape, fixed, pipeline_mode=pl.Buffered(1)),
            pl.BlockSpec((tm, d), row),
            pl.BlockSpec((1, d), fixed),
        ],
        out_specs=pl.BlockSpec((tm, d), row),
        out_shape=jax.ShapeDtypeStruct((s, d), F32),
        scratch_shapes=[pltpu.VMEM(w_out.shape, BF16)],
        compiler_params=pltpu.CompilerParams(
            dimension_semantics=("arbitrary",), vmem_limit_bytes=VMEM_LIMIT_BYTES),
        name="out_proj",
    )(mix_a, mix_b, w_out, x, norm_w)


def _rotate_half_cols(w):
    w1, w2 = jnp.split(w, 2, axis=-1)
    return jnp.concatenate([-w2, w1], axis=-1)


def _prep_q_up_weight(w_q_up):
    r = w_q_up.shape[0]
    w = w_q_up.reshape(r, MLA_HEADS, MLA_QK_DIM)
    nope, rope = w[:, :, :MLA_NOPE_DIM], w[:, :, MLA_NOPE_DIM:]
    groups = [nope, rope, _rotate_half_cols(rope)]
    return jnp.concatenate([g.reshape(r, -1) for g in groups], axis=1).astype(BF16)


def _prep_kv_up_weight(w_kv_up):
    r = w_kv_up.shape[0]
    w = w_kv_up.reshape(r, MLA_HEADS, MLA_NOPE_DIM + MLA_V_DIM)
    k_nope = w[:, :, :MLA_NOPE_DIM].reshape(r, MLA_HEADS * MLA_NOPE_DIM)
    v = w[:, :, MLA_NOPE_DIM:].reshape(r, MLA_WIDTH)
    return jnp.concatenate([k_nope, v], axis=1).astype(BF16)


def _layer(x, cos, sin, pre_norm_w, w_in, q_norm_w, w_q_up, kv_norm_w, w_kv_up, w_out, post_norm_w):
    main, tail = _in_proj(x, pre_norm_w.reshape(1, -1), w_in.T, tm=1024)
    q_mla, k_mla, v_mla = _mla_prep(tail, cos, sin, q_norm_w.reshape(1, -1), kv_norm_w.reshape(1, -1),
                                    _prep_q_up_weight(w_q_up), _prep_kv_up_weight(w_kv_up), tm=1024)
    mix_a = _sb_attn(main, tq=256, tk=256, heads_per_step=8)
    mix_b = _mla_attn(q_mla, k_mla, v_mla, main, tk=512, heads_per_step=2)
    return _out_proj(mix_a, mix_b, w_out, x, post_norm_w.reshape(1, -1), tm=512)


def kernel(x, positions, pre_norm_w, w_in, q_norm_w, w_q_up, kv_norm_w, w_kv_up, w_out, post_norm_w):
    batch, depth = x.shape[0], pre_norm_w.shape[0]
    outs = []
    for b in range(batch):
        xb = x[b]
        cos, sin = _rope_tables(positions[b])
        for i in range(depth):
            xb = _layer(xb, cos, sin, pre_norm_w[i], w_in[i], q_norm_w[i], w_q_up[i],
                        kv_norm_w[i], w_kv_up[i], w_out[i], post_norm_w[i])
        outs.append(xb)
    return outs[0][None] if batch == 1 else jnp.stack(outs, axis=0)
```

```python
import functools
import math

import numpy as np
import jax
import jax.numpy as jnp
from jax import lax
from jax.experimental import pallas as pl
from jax.experimental.pallas import tpu as pltpu

F32 = jnp.float32
BF16 = jnp.bfloat16

LANES = 128
VMEM_LIMIT_BYTES = 56 * 1024 * 1024

CHUNK = 64
EPS = 1e-6
SB_HEADS = 8
SB_HEAD_DIM = 128
SB_WIDTH = SB_HEADS * SB_HEAD_DIM
MLA_HEADS = 8
MLA_NOPE_DIM = 128
MLA_ROPE_DIM = 64
MLA_QK_DIM = MLA_NOPE_DIM + MLA_ROPE_DIM
MLA_V_DIM = 128
MLA_WIDTH = MLA_HEADS * MLA_V_DIM
Q_LORA_RANK = 512
KV_LORA_RANK = 256
ROPE_THETA = 10000.0
HALF_ROPE = MLA_ROPE_DIM // 2
MLA_QK_PAD = 2 * LANES
LOG2_E = math.log2(math.e)
SB_SKIP_LOG2 = -150.0

MAIN_WIDTH = 4 * SB_WIDTH + MLA_WIDTH
TAIL_WIDTH = Q_LORA_RANK + KV_LORA_RANK + 2 * LANES
MLA_G_BLK = 4 * SB_WIDTH // LANES


def _rms_norm(x, w):
    return x * lax.rsqrt(jnp.mean(x * x, axis=-1, keepdims=True) + EPS) * w


def _silu(g):
    return g * (1.0 / (1.0 + jnp.exp(-g)))


def _dot_nt(a, b):
    return lax.dot_general(a, b, (((1,), (1,)), ((), ())), preferred_element_type=F32)


IN_TILE = 1024
N_SB = 4 * SB_WIDTH // IN_TILE
N_MAIN = MAIN_WIDTH // IN_TILE
N_IN_TILES = N_MAIN + TAIL_WIDTH // IN_TILE
C_Q_ROW = 4 * SB_WIDTH
MLA_GATE_ROW = C_Q_ROW + Q_LORA_RANK + KV_LORA_RANK + MLA_ROPE_DIM
assert SB_WIDTH % IN_TILE == 0 and MLA_WIDTH % IN_TILE == 0 and TAIL_WIDTH % IN_TILE == 0


def _in_proj_kernel(x_ref, nw_ref, wt_ref, main_ref, tail_ref, h_ref, *, q_scale):
    j = pl.program_id(1)

    @pl.when(j == 0)
    def _():
        h_ref[...] = _rms_norm(x_ref[...], nw_ref[...]).astype(BF16)

    def proj():
        return _dot_nt(h_ref[...], wt_ref[...].astype(BF16))

    @pl.when(j < SB_WIDTH // IN_TILE)
    def _():
        main_ref[...] = (proj() * q_scale).astype(BF16)

    @pl.when((j >= SB_WIDTH // IN_TILE) & (j < N_MAIN))
    def _():
        main_ref[...] = proj().astype(BF16)

    @pl.when(j >= N_MAIN)
    def _():
        tail_ref[...] = proj()


def _in_proj(x, norm_w, w_in_t, *, tm):
    s, d = x.shape
    kern = functools.partial(_in_proj_kernel, q_scale=LOG2_E / math.sqrt(SB_HEAD_DIM))

    def weight_row(i, j):
        unit = MLA_ROPE_DIM
        step = IN_TILE // unit
        row = jnp.where(j < N_SB, j * step,
                        jnp.where(j < N_MAIN, MLA_GATE_ROW // unit + (j - N_SB) * step,
                                  C_Q_ROW // unit + (j - N_MAIN) * step))
        return (row * unit, 0)

    return pl.pallas_call(
        kern,
        grid=(s // tm, N_IN_TILES),
        in_specs=[
            pl.BlockSpec((tm, d), lambda i, j: (i, 0)),
            pl.BlockSpec((1, d), lambda i, j: (0, 0)),
            pl.BlockSpec((pl.Element(IN_TILE), pl.Element(d)), weight_row),
        ],
        out_specs=[
            pl.BlockSpec((tm, IN_TILE), lambda i, j: (i, jnp.minimum(j, N_MAIN - 1))),
            pl.BlockSpec((tm, IN_TILE), lambda i, j: (i, jnp.maximum(j - N_MAIN, 0))),
        ],
        out_shape=[jax.ShapeDtypeStruct((s, MAIN_WIDTH), BF16),
                   jax.ShapeDtypeStruct((s, TAIL_WIDTH), F32)],
        scratch_shapes=[pltpu.VMEM((tm, d), BF16)],
        compiler_params=pltpu.CompilerParams(
            dimension_semantics=("arbitrary", "arbitrary"), vmem_limit_bytes=VMEM_LIMIT_BYTES),
        name="in_proj",
    )(x, norm_w, w_in_t)


def _rope_table_kernel(pos_ref, freq_ref, cos_ref, sin_ref):
    ang = freq_ref[...] * pos_ref[...].astype(F32)
    cos = jnp.cos(ang)
    sin = jnp.sin(ang)
    cos_ref[...] = jnp.concatenate([cos] * (LANES // HALF_ROPE), axis=0)
    sin_ref[...] = jnp.concatenate([sin] * (LANES // HALF_ROPE), axis=0)


def _rope_tables(positions):
    s = positions.shape[0]
    inv_freq = ROPE_THETA ** (-jnp.arange(0, MLA_ROPE_DIM, 2, dtype=F32) / MLA_ROPE_DIM)
    shape = jax.ShapeDtypeStruct((LANES, s), F32)
    return pl.pallas_call(_rope_table_kernel, out_shape=[shape, shape], name="rope_tab")(
        positions.reshape(1, s), inv_freq.reshape(HALF_ROPE, 1))


def _mla_prep_kernel(tail_ref, cos_ref, sin_ref, qnw_ref, kvnw_ref, wq_ref, wkv_ref,
                     q_ref, k_ref, v_ref, *, q_scale):
    cos = cos_ref[...].T
    sin = sin_ref[...].T
    lane = lax.broadcasted_iota(jnp.int32, cos.shape, 1)
    first_half = lane < MLA_ROPE_DIM
    c_q = tail_ref[:, 0:Q_LORA_RANK]
    hq = _rms_norm(c_q, qnw_ref[...]).astype(BF16)
    zq = jnp.dot(hq, wq_ref[...], preferred_element_type=F32)
    rope0 = MLA_HEADS * MLA_NOPE_DIM
    rot0 = rope0 + MLA_HEADS * MLA_ROPE_DIM
    for pair in range(MLA_HEADS // 2):
        lanes = slice(pair * LANES, (pair + 1) * LANES)
        roped = (zq[:, rope0:rot0][:, lanes] * cos + zq[:, rot0:][:, lanes] * sin)
        for h, mine in ((2 * pair, roped), (2 * pair + 1, pltpu.roll(roped, MLA_ROPE_DIM, axis=1))):
            lo = h * MLA_QK_PAD
            q_ref[:, lo:lo + LANES] = (zq[:, h * LANES:(h + 1) * LANES] * q_scale).astype(BF16)
            q_ref[:, lo + LANES:lo + 2 * LANES] = (
                jnp.where(first_half, mine, 0.0) * q_scale).astype(BF16)

    c_kv = tail_ref[:, Q_LORA_RANK:Q_LORA_RANK + KV_LORA_RANK]
    hkv = _rms_norm(c_kv, kvnw_ref[...]).astype(BF16)
    zkv = jnp.dot(hkv, wkv_ref[...], preferred_element_type=F32)
    kr0 = Q_LORA_RANK + KV_LORA_RANK
    kr = tail_ref[:, kr0:kr0 + LANES]
    rotated = jnp.where(lane < HALF_ROPE,
                        -pltpu.roll(kr, LANES - HALF_ROPE, axis=1),
                        pltpu.roll(kr, HALF_ROPE, axis=1))
    k_rot = jnp.where(first_half, kr * cos + rotated * sin, 0.0).astype(BF16)
    for h in range(MLA_HEADS):
        lo = h * MLA_QK_PAD
        k_ref[:, lo:lo + LANES] = zkv[:, h * LANES:(h + 1) * LANES].astype(BF16)
        k_ref[:, lo + LANES:lo + 2 * LANES] = k_rot
    v_ref[...] = zkv[:, MLA_HEADS * MLA_NOPE_DIM:].astype(BF16)


def _mla_prep(tail, cos, sin, q_norm_w, kv_norm_w, wq, wkv, *, tm):
    s = tail.shape[0]
    row = lambda i: (i, 0)
    fixed = lambda i: (0, 0)
    kern = functools.partial(_mla_prep_kernel, q_scale=LOG2_E / math.sqrt(MLA_QK_DIM))
    return pl.pallas_call(
        kern,
        grid=(s // tm,),
        in_specs=[
            pl.BlockSpec((tm, TAIL_WIDTH), row),
            pl.BlockSpec((LANES, tm), lambda i: (0, i)),
            pl.BlockSpec((LANES, tm), lambda i: (0, i)),
            pl.BlockSpec(q_norm_w.shape, fixed),
            pl.BlockSpec(kv_norm_w.shape, fixed),
            pl.BlockSpec(wq.shape, fixed),
            pl.BlockSpec(wkv.shape, fixed),
        ],
        out_specs=[
            pl.BlockSpec((tm, MLA_HEADS * MLA_QK_PAD), row),
            pl.BlockSpec((tm, MLA_HEADS * MLA_QK_PAD), row),
            pl.BlockSpec((tm, MLA_WIDTH), row),
        ],
        out_shape=[jax.ShapeDtypeStruct((s, MLA_HEADS * MLA_QK_PAD), BF16),
                   jax.ShapeDtypeStruct((s, MLA_HEADS * MLA_QK_PAD), BF16),
                   jax.ShapeDtypeStruct((s, MLA_WIDTH), BF16)],
        compiler_params=pltpu.CompilerParams(
            dimension_semantics=("arbitrary",), vmem_limit_bytes=VMEM_LIMIT_BYTES),
        name="mla_prep",
    )(tail, cos, sin, q_norm_w, kv_norm_w, wq, wkv)


def _sb_attn_kernel(q_ref, k_ref, v_ref, g_ref, o_ref, tri_ref, acc_ref, carry_ref, *, tq, tk):
    i = pl.program_id(1)
    n_heads = q_ref.shape[1] // SB_HEAD_DIM

    rows = lax.broadcasted_iota(jnp.int32, (tk, tk), 0)
    cols = lax.broadcasted_iota(jnp.int32, (tk, tk), 1)
    tri = jnp.where(rows >= cols, -1.0, 0.0).astype(BF16)
    tri_ref[0:tk, :] = tri
    tri_ref[tk:2 * tk, :] = tri
    acc_ref[...] = jnp.zeros_like(acc_ref)
    carry_ref[...] = jnp.zeros_like(carry_ref)

    def block(start, masked):
        heads = range(n_heads)
        lanes = [slice(h * SB_HEAD_DIM, (h + 1) * SB_HEAD_DIM) for h in heads]
        z = [_dot_nt(q_ref[:, lanes[h]], k_ref[pl.ds(start, tk), lanes[h]]) for h in heads]
        if masked:
            t_idx = i * tq + lax.broadcasted_iota(jnp.int32, (tq, tk), 0)
            s_idx = start + lax.broadcasted_iota(jnp.int32, (tq, tk), 1)
            before = s_idx < t_idx
            z = [jnp.where(before, z[h], -jnp.inf) for h in heads]
        sp = [jnp.maximum(z[h], 0.0) + jnp.log2(1.0 + jnp.exp2(-jnp.abs(z[h]))) for h in heads]
        tri = tri_ref[...]
        sums = []
        for h in heads:
            hi = sp[h].astype(BF16)
            lo = (sp[h] - hi.astype(F32)).astype(BF16)
            sums.append(jnp.dot(jnp.concatenate([hi, lo], axis=1), tri,
                                preferred_element_type=F32))
        carry = [carry_ref[:, lanes[h]] for h in heads]
        a = [jnp.exp2(z[h] + sums[h] + jnp.concatenate([carry[h]] * (tk // LANES), axis=1))
             for h in heads]
        for h in heads:
            acc_ref[:, lanes[h]] += jnp.dot(a[h].astype(BF16), v_ref[pl.ds(start, tk), lanes[h]],
                                            preferred_element_type=F32)
            carry_ref[:, lanes[h]] = carry[h] + jnp.broadcast_to(sums[h][:, 0:1], (tq, LANES))

    n_diag = tq // tk
    for d in reversed(range(n_diag)):
        block(pl.multiple_of(i * tq + d * tk, tk), True)

    def live():
        return (jnp.max(carry_ref[...]) > SB_SKIP_LOG2).astype(jnp.int32)

    def cond(state):
        j, alive = state
        return (j >= 0) & (alive > 0)

    def body(state):
        j, _ = state
        block(pl.multiple_of(j * tk, tk), False)
        return j - 1, live()

    lax.while_loop(cond, body, (i * n_diag - 1, jnp.int32(1)))
    o_ref[...] = (acc_ref[...] * _silu(g_ref[...].astype(F32))).astype(o_ref.dtype)


def _sb_attn(main, *, tq, tk, heads_per_step):
    s = main.shape[0]
    n_groups = SB_HEADS // heads_per_step
    width = heads_per_step * SB_HEAD_DIM
    kern = functools.partial(_sb_attn_kernel, tq=tq, tk=tk)
    resident = pl.Buffered(1) if n_groups == 1 else None
    return pl.pallas_call(
        kern,
        grid=(n_groups, s // tq),
        in_specs=[
            pl.BlockSpec((tq, width), lambda g, i: (i, g)),
            pl.BlockSpec((s, width), lambda g, i: (0, n_groups + g), pipeline_mode=resident),
            pl.BlockSpec((s, width), lambda g, i: (0, 2 * n_groups + g), pipeline_mode=resident),
            pl.BlockSpec((tq, width), lambda g, i: (i, 3 * n_groups + g)),
        ],
        out_specs=pl.BlockSpec((tq, width), lambda g, i: (i, g)),
        out_shape=jax.ShapeDtypeStruct((s, SB_WIDTH), BF16),
        scratch_shapes=[pltpu.VMEM((2 * tk, tk), BF16),
                        pltpu.VMEM((tq, width), F32),
                        pltpu.VMEM((tq, width), F32)],
        compiler_params=pltpu.CompilerParams(
            dimension_semantics=("arbitrary", "arbitrary"), vmem_limit_bytes=VMEM_LIMIT_BYTES),
        name="sb_attn",
    )(main, main, main, main)


def _mla_attn_kernel(q_ref, k_ref, v_ref, g_ref, o_ref, s_ref, m_ref, acc_ref, *, tk):
    i = pl.program_id(1)
    tq = 2 * tk
    n_col = tk // LANES
    n_heads = v_ref.shape[1] // MLA_V_DIM
    heads = range(n_heads)
    row_tile = 512
    top, bottom, both = slice(0, tk), slice(tk, tq), slice(0, tq)
    ones = jnp.ones((tk, LANES), BF16)
    m_ref[...] = jnp.full_like(m_ref, -jnp.inf)
    acc_ref[...] = jnp.zeros_like(acc_ref)

    def qk_lanes(head):
        return slice(head * MLA_QK_PAD, (head + 1) * MLA_QK_PAD)

    def v_lanes(head):
        return slice(head * MLA_V_DIM, (head + 1) * MLA_V_DIM)

    def scores(head, blk, slot, rows):
        start = pl.multiple_of(blk * tk, tk)
        s_ref[head, slot, rows] = _dot_nt(q_ref[rows, qk_lanes(head)],
                                          k_ref[pl.ds(start, tk), qk_lanes(head)])

    def tiles(blk, slot, rows, masked, next_blk=None):
        return [(head, blk, slot, slice(r0, r0 + row_tile), r0 - rows.start, masked, next_blk)
                for r0 in range(rows.start, rows.stop, row_tile) for head in heads]

    def softmax_stage(item):
        head, blk, slot, rows, row_offset, masked, _ = item

        def column(c):
            s = s_ref[head, slot, rows, c * LANES:(c + 1) * LANES]
            if masked:
                t_chunk = (row_offset + lax.broadcasted_iota(jnp.int32, (row_tile, LANES), 0)) // CHUNK
                s_chunk = (c * LANES + lax.broadcasted_iota(jnp.int32, (row_tile, LANES), 1)) // CHUNK
                s = jnp.where(s_chunk <= t_chunk, s, -jnp.inf)
            return s

        col_max = column(0)
        for c in range(1, n_col):
            col_max = jnp.maximum(col_max, column(c))
        m_prev = m_ref[head, rows]
        m_new = jnp.maximum(m_prev, jnp.max(col_max, axis=-1, keepdims=True))
        m_ref[head, rows] = m_new
        p = jnp.concatenate([jnp.exp2((column(c) - m_new).astype(BF16)) for c in range(n_col)],
                            axis=1)
        return head, blk, rows, jnp.exp2(m_prev - m_new), p

    def value_stage(head, blk, rows, alpha, p):
        start = pl.multiple_of(blk * tk, tk)
        v_ones = jnp.concatenate([v_ref[pl.ds(start, tk), v_lanes(head)], ones], axis=1)
        acc_ref[head, rows] = (jnp.concatenate([alpha, alpha], axis=1) * acc_ref[head, rows]
                               + jnp.dot(p, v_ones, preferred_element_type=F32))

    def fold(items):
        for item in items:
            head, _, slot, rows, _, _, next_blk = item
            if next_blk is not None:
                scores(head, next_blk, 1 - slot, rows)
            value_stage(*softmax_stage(item))

    for head in heads:
        scores(head, 0, 0, both)

    def pair(n, _):
        blk = 2 * n
        fold(tiles(blk, 0, both, False, next_blk=blk + 1)
             + tiles(blk + 1, 1, both, False, next_blk=blk + 2))
        return 0

    lax.fori_loop(0, i, pair, 0)

    for head in heads:
        scores(head, 2 * i + 1, 1, bottom)
    fold(tiles(2 * i, 0, top, True) + tiles(2 * i, 0, bottom, False)
         + tiles(2 * i + 1, 1, bottom, True))

    for head in heads:
        out = acc_ref[head, :, 0:MLA_V_DIM] * (1.0 / acc_ref[head, :, MLA_V_DIM:])
        gate = g_ref[:, v_lanes(head)].astype(F32)
        o_ref[:, v_lanes(head)] = (out * _silu(gate)).astype(o_ref.dtype)


def _mla_attn(q, k, v, main, *, tk, heads_per_step):
    s = q.shape[0]
    tq = 2 * tk
    n_groups = MLA_HEADS // heads_per_step
    qk_width = heads_per_step * MLA_QK_PAD
    v_width = heads_per_step * MLA_V_DIM
    gate_block = MLA_G_BLK // heads_per_step
    kern = functools.partial(_mla_attn_kernel, tk=tk)
    return pl.pallas_call(
        kern,
        grid=(n_groups, s // tq),
        in_specs=[
            pl.BlockSpec((tq, qk_width), lambda g, i: (i, g)),
            pl.BlockSpec((s, qk_width), lambda g, i: (0, g)),
            pl.BlockSpec((s, v_width), lambda g, i: (0, g)),
            pl.BlockSpec((tq, v_width), lambda g, i: (i, gate_block + g)),
        ],
        out_specs=pl.BlockSpec((tq, v_width), lambda g, i: (i, g)),
        out_shape=jax.ShapeDtypeStruct((s, MLA_WIDTH), BF16),
        scratch_shapes=[pltpu.VMEM((heads_per_step, 2, tq, tk), F32),
                        pltpu.VMEM((heads_per_step, tq, LANES), F32),
                        pltpu.VMEM((heads_per_step, tq, MLA_V_DIM + LANES), F32)],
        compiler_params=pltpu.CompilerParams(
            dimension_semantics=("arbitrary", "arbitrary"), vmem_limit_bytes=VMEM_LIMIT_BYTES),
        name="mla_attn",
    )(q, k, v, main)


def _out_proj_kernel(a_ref, b_ref, w_ref, x_ref, nw_ref, o_ref, wb_ref):
    @pl.when(pl.program_id(0) == 0)
    def _():
        wb_ref[...] = w_ref[...].astype(BF16)

    y = (jnp.dot(a_ref[...], wb_ref[0:SB_WIDTH, :], preferred_element_type=F32)
         + jnp.dot(b_ref[...], wb_ref[SB_WIDTH:, :], preferred_element_type=F32))
    o_ref[...] = x_ref[...] + _rms_norm(y, nw_ref[...])


def _out_proj(mix_a, mix_b, w_out, x, norm_w, *, tm):
    s, d = x.shape
    row = lambda i: (i, 0)
    fixed = lambda i: (0, 0)
    return pl.pallas_call(
        _out_proj_kernel,
        grid=(s // tm,),
        in_specs=[
            pl.BlockSpec((tm, SB_WIDTH), row),
            pl.BlockSpec((tm, MLA_WIDTH), row),
            pl.BlockSpec(w_out.shape, fixed, pipeline_mode=pl.Buffered(1)),
            pl.BlockSpec((tm, d), row),
            pl.BlockSpec((1, d), fixed),
        ],
        out_specs=pl.BlockSpec((tm, d), row),
        out_shape=jax.ShapeDtypeStruct((s, d), F32),
        scratch_shapes=[pltpu.VMEM(w_out.shape, BF16)],
        compiler_params=pltpu.CompilerParams(
            dimension_semantics=("arbitrary",), vmem_limit_bytes=VMEM_LIMIT_BYTES),
        name="out_proj",
    )(mix_a, mix_b, w_out, x, norm_w)


def _rotate_half_cols(w):
    w1, w2 = jnp.split(w, 2, axis=-1)
    return jnp.concatenate([-w2, w1], axis=-1)


def _prep_q_up_weight(w_q_up):
    r = w_q_up.shape[0]
    w = w_q_up.reshape(r, MLA_HEADS, MLA_QK_DIM)
    nope, rope = w[:, :, :MLA_NOPE_DIM], w[:, :, MLA_NOPE_DIM:]
    groups = [nope, rope, _rotate_half_cols(rope)]
    return jnp.concatenate([g.reshape(r, -1) for g in groups], axis=1).astype(BF16)


def _prep_kv_up_weight(w_kv_up):
    r = w_kv_up.shape[0]
    w = w_kv_up.reshape(r, MLA_HEADS, MLA_NOPE_DIM + MLA_V_DIM)
    k_nope = w[:, :, :MLA_NOPE_DIM].reshape(r, MLA_HEADS * MLA_NOPE_DIM)
    v = w[:, :, MLA_NOPE_DIM:].reshape(r, MLA_WIDTH)
    return jnp.concatenate([k_nope, v], axis=1).astype(BF16)


def _layer(x, cos, sin, pre_norm_w, w_in, q_norm_w, w_q_up, kv_norm_w, w_kv_up, w_out, post_norm_w):
    main, tail = _in_proj(x, pre_norm_w.reshape(1, -1), w_in.T, tm=1024)
    q_mla, k_mla, v_mla = _mla_prep(tail, cos, sin, q_norm_w.reshape(1, -1), kv_norm_w.reshape(1, -1),
                                    _prep_q_up_weight(w_q_up), _prep_kv_up_weight(w_kv_up), tm=1024)
    mix_a = _sb_attn(main, tq=256, tk=256, heads_per_step=8)
    mix_b = _mla_attn(q_mla, k_mla, v_mla, main, tk=512, heads_per_step=2)
    return _out_proj(mix_a, mix_b, w_out, x, post_norm_w.reshape(1, -1), tm=512)


def kernel(x, positions, pre_norm_w, w_in, q_norm_w, w_q_up, kv_norm_w, w_kv_up, w_out, post_norm_w):
    batch, depth = x.shape[0], pre_norm_w.shape[0]
    outs = []
    for b in range(batch):
        xb = x[b]
        cos, sin = _rope_tables(positions[b])
        for i in range(depth):
            xb = _layer(xb, cos, sin, pre_norm_w[i], w_in[i], q_norm_w[i], w_q_up[i],
                        kv_norm_w[i], w_kv_up[i], w_out[i], post_norm_w[i])
        outs.append(xb)
    return outs[0][None] if batch == 1 else jnp.stack(outs, axis=0)
```

```python
import functools
import math

import numpy as np
import jax
import jax.numpy as jnp
from jax import lax
from jax.experimental import pallas as pl
from jax.experimental.pallas import tpu as pltpu

F32 = jnp.float32
BF16 = jnp.bfloat16

LANES = 128
VMEM_LIMIT_BYTES = 56 * 1024 * 1024

CHUNK = 64
EPS = 1e-6
SB_HEADS = 8
SB_HEAD_DIM = 128
SB_WIDTH = SB_HEADS * SB_HEAD_DIM
MLA_HEADS = 8
MLA_NOPE_DIM = 128
MLA_ROPE_DIM = 64
MLA_QK_DIM = MLA_NOPE_DIM + MLA_ROPE_DIM
MLA_V_DIM = 128
MLA_WIDTH = MLA_HEADS * MLA_V_DIM
Q_LORA_RANK = 512
KV_LORA_RANK = 256
ROPE_THETA = 10000.0
HALF_ROPE = MLA_ROPE_DIM // 2
MLA_QK_PAD = 2 * LANES
LOG2_E = math.log2(math.e)
SB_SKIP_LOG2 = -150.0

MAIN_WIDTH = 4 * SB_WIDTH + MLA_WIDTH
TAIL_WIDTH = Q_LORA_RANK + KV_LORA_RANK + 2 * LANES
MLA_G_BLK = 4 * SB_WIDTH // LANES


def _rms_norm(x, w):
    return x * lax.rsqrt(jnp.mean(x * x, axis=-1, keepdims=True) + EPS) * w


def _silu(g):
    return g * (1.0 / (1.0 + jnp.exp(-g)))


def _dot_nt(a, b):
    return lax.dot_general(a, b, (((1,), (1,)), ((), ())), preferred_element_type=F32)


IN_TILE = 1024
N_SB = 4 * SB_WIDTH // IN_TILE
N_MAIN = MAIN_WIDTH // IN_TILE
N_IN_TILES = N_MAIN + TAIL_WIDTH // IN_TILE
C_Q_ROW = 4 * SB_WIDTH
MLA_GATE_ROW = C_Q_ROW + Q_LORA_RANK + KV_LORA_RANK + MLA_ROPE_DIM
assert SB_WIDTH % IN_TILE == 0 and MLA_WIDTH % IN_TILE == 0 and TAIL_WIDTH % IN_TILE == 0


def _in_proj_kernel(x_ref, nw_ref, wt_ref, main_ref, tail_ref, h_ref, *, q_scale):
    j = pl.program_id(1)

    @pl.when(j == 0)
    def _():
        h_ref[...] = _rms_norm(x_ref[...], nw_ref[...]).astype(BF16)

    def proj():
        return _dot_nt(h_ref[...], wt_ref[...].astype(BF16))

    @pl.when(j < SB_WIDTH // IN_TILE)
    def _():
        main_ref[...] = (proj() * q_scale).astype(BF16)

    @pl.when((j >= SB_WIDTH // IN_TILE) & (j < N_MAIN))
    def _():
        main_ref[...] = proj().astype(BF16)

    @pl.when(j >= N_MAIN)
    def _():
        tail_ref[...] = proj()


def _in_proj(x, norm_w, w_in_t, *, tm):
    s, d = x.shape
    kern = functools.partial(_in_proj_kernel, q_scale=LOG2_E / math.sqrt(SB_HEAD_DIM))

    def weight_row(i, j):
        unit = MLA_ROPE_DIM
        step = IN_TILE // unit
        row = jnp.where(j < N_SB, j * step,
                        jnp.where(j < N_MAIN, MLA_GATE_ROW // unit + (j - N_SB) * step,
                                  C_Q_ROW // unit + (j - N_MAIN) * step))
        return (row * unit, 0)

    return pl.pallas_call(
        kern,
        grid=(s // tm, N_IN_TILES),
        in_specs=[
            pl.BlockSpec((tm, d), lambda i, j: (i, 0)),
            pl.BlockSpec((1, d), lambda i, j: (0, 0)),
            pl.BlockSpec((pl.Element(IN_TILE), pl.Element(d)), weight_row),
        ],
        out_specs=[
            pl.BlockSpec((tm, IN_TILE), lambda i, j: (i, jnp.minimum(j, N_MAIN - 1))),
            pl.BlockSpec((tm, IN_TILE), lambda i, j: (i, jnp.maximum(j - N_MAIN, 0))),
        ],
        out_shape=[jax.ShapeDtypeStruct((s, MAIN_WIDTH), BF16),
                   jax.ShapeDtypeStruct((s, TAIL_WIDTH), F32)],
        scratch_shapes=[pltpu.VMEM((tm, d), BF16)],
        compiler_params=pltpu.CompilerParams(
            dimension_semantics=("arbitrary", "arbitrary"), vmem_limit_bytes=VMEM_LIMIT_BYTES),
        name="in_proj",
    )(x, norm_w, w_in_t)


def _rope_table_kernel(pos_ref, freq_ref, cos_ref, sin_ref):
    ang = freq_ref[...] * pos_ref[...].astype(F32)
    cos = jnp.cos(ang)
    sin = jnp.sin(ang)
    cos_ref[...] = jnp.concatenate([cos] * (LANES // HALF_ROPE), axis=0)
    sin_ref[...] = jnp.concatenate([sin] * (LANES // HALF_ROPE), axis=0)


def _rope_tables(positions):
    s = positions.shape[0]
    inv_freq = ROPE_THETA ** (-jnp.arange(0, MLA_ROPE_DIM, 2, dtype=F32) / MLA_ROPE_DIM)
    shape = jax.ShapeDtypeStruct((LANES, s), F32)
    return pl.pallas_call(_rope_table_kernel, out_shape=[shape, shape], name="rope_tab")(
        positions.reshape(1, s), inv_freq.reshape(HALF_ROPE, 1))


def _mla_prep_kernel(tail_ref, cos_ref, sin_ref, qnw_ref, kvnw_ref, wq_ref, wkv_ref,
                     q_ref, k_ref, v_ref, *, q_scale):
    cos = cos_ref[...].T
    sin = sin_ref[...].T
    lane = lax.broadcasted_iota(jnp.int32, cos.shape, 1)
    first_half = lane < MLA_ROPE_DIM
    c_q = tail_ref[:, 0:Q_LORA_RANK]
    hq = _rms_norm(c_q, qnw_ref[...]).astype(BF16)
    zq = jnp.dot(hq, wq_ref[...], preferred_element_type=F32)
    rope0 = MLA_HEADS * MLA_NOPE_DIM
    rot0 = rope0 + MLA_HEADS * MLA_ROPE_DIM
    for pair in range(MLA_HEADS // 2):
        lanes = slice(pair * LANES, (pair + 1) * LANES)
        roped = (zq[:, rope0:rot0][:, lanes] * cos + zq[:, rot0:][:, lanes] * sin)
        for h, mine in ((2 * pair, roped), (2 * pair + 1, pltpu.roll(roped, MLA_ROPE_DIM, axis=1))):
            lo = h * MLA_QK_PAD
            q_ref[:, lo:lo + LANES] = (zq[:, h * LANES:(h + 1) * LANES] * q_scale).astype(BF16)
            q_ref[:, lo + LANES:lo + 2 * LANES] = (
                jnp.where(first_half, mine, 0.0) * q_scale).astype(BF16)

    c_kv = tail_ref[:, Q_LORA_RANK:Q_LORA_RANK + KV_LORA_RANK]
    hkv = _rms_norm(c_kv, kvnw_ref[...]).astype(BF16)
    zkv = jnp.dot(hkv, wkv_ref[...], preferred_element_type=F32)
    kr0 = Q_LORA_RANK + KV_LORA_RANK
    kr = tail_ref[:, kr0:kr0 + LANES]
    rotated = jnp.where(lane < HALF_ROPE,
                        -pltpu.roll(kr, LANES - HALF_ROPE, axis=1),
                        pltpu.roll(kr, HALF_ROPE, axis=1))
    k_rot = jnp.where(first_half, kr * cos + rotated * sin, 0.0).astype(BF16)
    for h in range(MLA_HEADS):
        lo = h * MLA_QK_PAD
        k_ref[:, lo:lo + LANES] = zkv[:, h * LANES:(h + 1) * LANES].astype(BF16)
        k_ref[:, lo + LANES:lo + 2 * LANES] = k_rot
    v_ref[...] = zkv[:, MLA_HEADS * MLA_NOPE_DIM:].astype(BF16)


def _mla_prep(tail, cos, sin, q_norm_w, kv_norm_w, wq, wkv, *, tm):
    s = tail.shape[0]
    row = lambda i: (i, 0)
    fixed = lambda i: (0, 0)
    kern = functools.partial(_mla_prep_kernel, q_scale=LOG2_E / math.sqrt(MLA_QK_DIM))
    return pl.pallas_call(
        kern,
        grid=(s // tm,),
        in_specs=[
            pl.BlockSpec((tm, TAIL_WIDTH), row),
            pl.BlockSpec((LANES, tm), lambda i: (0, i)),
            pl.BlockSpec((LANES, tm), lambda i: (0, i)),
            pl.BlockSpec(q_norm_w.shape, fixed),
            pl.BlockSpec(kv_norm_w.shape, fixed),
            pl.BlockSpec(wq.shape, fixed),
            pl.BlockSpec(wkv.shape, fixed),
        ],
        out_specs=[
            pl.BlockSpec((tm, MLA_HEADS * MLA_QK_PAD), row),
            pl.BlockSpec((tm, MLA_HEADS * MLA_QK_PAD), row),
            pl.BlockSpec((tm, MLA_WIDTH), row),
        ],
        out_shape=[jax.ShapeDtypeStruct((s, MLA_HEADS * MLA_QK_PAD), BF16),
                   jax.ShapeDtypeStruct((s, MLA_HEADS * MLA_QK_PAD), BF16),
                   jax.ShapeDtypeStruct((s, MLA_WIDTH), BF16)],
        compiler_params=pltpu.CompilerParams(
            dimension_semantics=("arbitrary",), vmem_limit_bytes=VMEM_LIMIT_BYTES),
        name="mla_prep",
    )(tail, cos, sin, q_norm_w, kv_norm_w, wq, wkv)


def _sb_attn_kernel(q_ref, k_ref, v_ref, g_ref, o_ref, tri_ref, acc_ref, carry_ref, *, tq, tk):
    i = pl.program_id(1)
    n_heads = q_ref.shape[1] // SB_HEAD_DIM

    rows = lax.broadcasted_iota(jnp.int32, (tk, tk), 0)
    cols = lax.broadcasted_iota(jnp.int32, (tk, tk), 1)
    tri = jnp.where(rows >= cols, -1.0, 0.0).astype(BF16)
    tri_ref[0:tk, :] = tri
    tri_ref[tk:2 * tk, :] = tri
    acc_ref[...] = jnp.zeros_like(acc_ref)
    carry_ref[...] = jnp.zeros_like(carry_ref)

    def block(start, masked):
        heads = range(n_heads)
        lanes = [slice(h * SB_HEAD_DIM, (h + 1) * SB_HEAD_DIM) for h in heads]
        z = [_dot_nt(q_ref[:, lanes[h]], k_ref[pl.ds(start, tk), lanes[h]]) for h in heads]
        if masked:
            t_idx = i * tq + lax.broadcasted_iota(jnp.int32, (tq, tk), 0)
            s_idx = start + lax.broadcasted_iota(jnp.int32, (tq, tk), 1)
            before = s_idx < t_idx
            z = [jnp.where(before, z[h], -jnp.inf) for h in heads]
        sp = [jnp.maximum(z[h], 0.0) + jnp.log2(1.0 + jnp.exp2(-jnp.abs(z[h]))) for h in heads]
        tri = tri_ref[...]
        sums = []
        for h in heads:
            hi = sp[h].astype(BF16)
            lo = (sp[h] - hi.astype(F32)).astype(BF16)
            sums.append(jnp.dot(jnp.concatenate([hi, lo], axis=1), tri,
                                preferred_element_type=F32))
        carry = [carry_ref[:, lanes[h]] for h in heads]
        a = [jnp.exp2(z[h] + sums[h] + jnp.concatenate([carry[h]] * (tk // LANES), axis=1))
             for h in heads]
        for h in heads:
            acc_ref[:, lanes[h]] += jnp.dot(a[h].astype(BF16), v_ref[pl.ds(start, tk), lanes[h]],
                                            preferred_element_type=F32)
            carry_ref[:, lanes[h]] = carry[h] + jnp.broadcast_to(sums[h][:, 0:1], (tq, LANES))

    n_diag = tq // tk
    for d in reversed(range(n_diag)):
        block(pl.multiple_of(i * tq + d * tk, tk), True)

    def live():
        return (jnp.max(carry_ref[...]) > SB_SKIP_LOG2).astype(jnp.int32)

    def cond(state):
        j, alive = state
        return (j >= 0) & (alive > 0)

    def body(state):
        j, _ = state
        block(pl.multiple_of(j * tk, tk), False)
        return j - 1, live()

    lax.while_loop(cond, body, (i * n_diag - 1, jnp.int32(1)))
    o_ref[...] = (acc_ref[...] * _silu(g_ref[...].astype(F32))).astype(o_ref.dtype)


def _sb_attn(main, *, tq, tk, heads_per_step):
    s = main.shape[0]
    n_groups = SB_HEADS // heads_per_step
    width = heads_per_step * SB_HEAD_DIM
    kern = functools.partial(_sb_attn_kernel, tq=tq, tk=tk)
    resident = pl.Buffered(1) if n_groups == 1 else None
    return pl.pallas_call(
        kern,
        grid=(n_groups, s // tq),
        in_specs=[
            pl.BlockSpec((tq, width), lambda g, i: (i, g)),
            pl.BlockSpec((s, width), lambda g, i: (0, n_groups + g), pipeline_mode=resident),
            pl.BlockSpec((s, width), lambda g, i: (0, 2 * n_groups + g), pipeline_mode=resident),
            pl.BlockSpec((tq, width), lambda g, i: (i, 3 * n_groups + g)),
        ],
        out_specs=pl.BlockSpec((tq, width), lambda g, i: (i, g)),
        out_shape=jax.ShapeDtypeStruct((s, SB_WIDTH), BF16),
        scratch_shapes=[pltpu.VMEM((2 * tk, tk), BF16),
                        pltpu.VMEM((tq, width), F32),
                        pltpu.VMEM((tq, width), F32)],
        compiler_params=pltpu.CompilerParams(
            dimension_semantics=("arbitrary", "arbitrary"), vmem_limit_bytes=VMEM_LIMIT_BYTES),
        name="sb_attn",
    )(main, main, main, main)


def _mla_attn_kernel(q_ref, k_ref, v_ref, g_ref, o_ref, s_ref, m_ref, acc_ref, *, tk):
    i = pl.program_id(1)
    tq = 2 * tk
    n_col = tk // LANES
    n_heads = v_ref.shape[1] // MLA_V_DIM
    heads = range(n_heads)
    row_tile = 512
    top, bottom, both = slice(0, tk), slice(tk, tq), slice(0, tq)
    ones = jnp.ones((tk, LANES), BF16)
    m_ref[...] = jnp.full_like(m_ref, -jnp.inf)
    acc_ref[...] = jnp.zeros_like(acc_ref)

    def qk_lanes(head):
        return slice(head * MLA_QK_PAD, (head + 1) * MLA_QK_PAD)

    def v_lanes(head):
        return slice(head * MLA_V_DIM, (head + 1) * MLA_V_DIM)

    def scores(head, blk, slot, rows):
        start = pl.multiple_of(blk * tk, tk)
        s_ref[head, slot, rows] = _dot_nt(q_ref[rows, qk_lanes(head)],
                                          k_ref[pl.ds(start, tk), qk_lanes(head)])

    def tiles(blk, slot, rows, masked, next_blk=None):
        return [(head, blk, slot, slice(r0, r0 + row_tile), r0 - rows.start, masked, next_blk)
                for r0 in range(rows.start, rows.stop, row_tile) for head in heads]

    def softmax_stage(item):
        head, blk, slot, rows, row_offset, masked, _ = item

        def column(c):
            s = s_ref[head, slot, rows, c * LANES:(c + 1) * LANES]
            if masked:
                t_chunk = (row_offset + lax.broadcasted_iota(jnp.int32, (row_tile, LANES), 0)) // CHUNK
                s_chunk = (c * LANES + lax.broadcasted_iota(jnp.int32, (row_tile, LANES), 1)) // CHUNK
                s = jnp.where(s_chunk <= t_chunk, s, -jnp.inf)
            return s

        col_max = column(0)
        for c in range(1, n_col):
            col_max = jnp.maximum(col_max, column(c))
        m_prev = m_ref[head, rows]
        m_new = jnp.maximum(m_prev, jnp.max(col_max, axis=-1, keepdims=True))
        m_ref[head, rows] = m_new
        p = jnp.concatenate([jnp.exp2((column(c) - m_new).astype(BF16)) for c in range(n_col)],
                            axis=1)
        return head, blk, rows, jnp.exp2(m_prev - m_new), p

    def value_stage(head, blk, rows, alpha, p):
        start = pl.multiple_of(blk * tk, tk)
        v_ones = jnp.concatenate([v_ref[pl.ds(start, tk), v_lanes(head)], ones], axis=1)
        acc_ref[head, rows] = (jnp.concatenate([alpha, alpha], axis=1) * acc_ref[head, rows]
                               + jnp.dot(p, v_ones, preferred_element_type=F32))

    def fold(items):
        for item in items:
            head, _, slot, rows, _, _, next_blk = item
            if next_blk is not None:
                scores(head, next_blk, 1 - slot, rows)
            value_stage(*softmax_stage(item))

    for head in heads:
        scores(head, 0, 0, both)

    def pair(blk):
        return (tiles(blk, 0, both, False, next_blk=blk + 1)
                + tiles(blk + 1, 1, both, False, next_blk=blk + 2))

    def two_pairs(n, _):
        fold(pair(4 * n) + pair(4 * n + 2))
        return 0

    lax.fori_loop(0, lax.shift_right_logical(i, 1), two_pairs, 0)

    @pl.when((i & 1) == 1)
    def _():
        fold(pair(2 * i - 2))

    for head in heads:
        scores(head, 2 * i + 1, 1, bottom)
    fold(tiles(2 * i, 0, top, True) + tiles(2 * i, 0, bottom, False)
         + tiles(2 * i + 1, 1, bottom, True))

    for head in heads:
        out = acc_ref[head, :, 0:MLA_V_DIM] * (1.0 / acc_ref[head, :, MLA_V_DIM:])
        gate = g_ref[:, v_lanes(head)].astype(F32)
        o_ref[:, v_lanes(head)] = (out * _silu(gate)).astype(o_ref.dtype)


def _mla_attn(q, k, v, main, *, tk, heads_per_step):
    s = q.shape[0]
    tq = 2 * tk
    n_groups = MLA_HEADS // heads_per_step
    qk_width = heads_per_step * MLA_QK_PAD
    v_width = heads_per_step * MLA_V_DIM
    gate_block = MLA_G_BLK // heads_per_step
    kern = functools.partial(_mla_attn_kernel, tk=tk)
    return pl.pallas_call(
        kern,
        grid=(n_groups, s // tq),
        in_specs=[
            pl.BlockSpec((tq, qk_width), lambda g, i: (i, g)),
            pl.BlockSpec((s, qk_width), lambda g, i: (0, g)),
            pl.BlockSpec((s, v_width), lambda g, i: (0, g)),
            pl.BlockSpec((tq, v_width), lambda g, i: (i, gate_block + g)),
        ],
        out_specs=pl.BlockSpec((tq, v_width), lambda g, i: (i, g)),
        out_shape=jax.ShapeDtypeStruct((s, MLA_WIDTH), BF16),
        scratch_shapes=[pltpu.VMEM((heads_per_step, 2, tq, tk), F32),
                        pltpu.VMEM((heads_per_step, tq, LANES), F32),
                        pltpu.VMEM((heads_per_step, tq, MLA_V_DIM + LANES), F32)],
        compiler_params=pltpu.CompilerParams(
            dimension_semantics=("arbitrary", "arbitrary"), vmem_limit_bytes=VMEM_LIMIT_BYTES),
        name="mla_attn",
    )(q, k, v, main)


def _out_proj_kernel(a_ref, b_ref, w_ref, x_ref, nw_ref, o_ref, wb_ref):
    @pl.when(pl.program_id(0) == 0)
    def _():
        wb_ref[...] = w_ref[...].astype(BF16)

    y = (jnp.dot(a_ref[...], wb_ref[0:SB_WIDTH, :], preferred_element_type=F32)
         + jnp.dot(b_ref[...], wb_ref[SB_WIDTH:, :], preferred_element_type=F32))
    o_ref[...] = x_ref[...] + _rms_norm(y, nw_ref[...])


def _out_proj(mix_a, mix_b, w_out, x, norm_w, *, tm):
    s, d = x.shape
    row = lambda i: (i, 0)
    fixed = lambda i: (0, 0)
    return pl.pallas_call(
        _out_proj_kernel,
        grid=(s // tm,),
        in_specs=[
            pl.BlockSpec((tm, SB_WIDTH), row),
            pl.BlockSpec((tm, MLA_WIDTH), row),
            pl.BlockSpec(w_out.shape, fixed, pipeline_mode=pl.Buffered(1)),
            pl.BlockSpec((tm, d), row),
            pl.BlockSpec((1, d), fixed),
        ],
        out_specs=pl.BlockSpec((tm, d), row),
        out_shape=jax.ShapeDtypeStruct((s, d), F32),
        scratch_shapes=[pltpu.VMEM(w_out.shape, BF16)],
        compiler_params=pltpu.CompilerParams(
            dimension_semantics=("arbitrary",), vmem_limit_bytes=VMEM_LIMIT_BYTES),
        name="out_proj",
    )(mix_a, mix_b, w_out, x, norm_w)


def _rotate_half_cols(w):
    w1, w2 = jnp.split(w, 2, axis=-1)
    return jnp.concatenate([-w2, w1], axis=-1)


def _prep_q_up_weight(w_q_up):
    r = w_q_up.shape[0]
    w = w_q_up.reshape(r, MLA_HEADS, MLA_QK_DIM)
    nope, rope = w[:, :, :MLA_NOPE_DIM], w[:, :, MLA_NOPE_DIM:]
    groups = [nope, rope, _rotate_half_cols(rope)]
    return jnp.concatenate([g.reshape(r, -1) for g in groups], axis=1).astype(BF16)


def _prep_kv_up_weight(w_kv_up):
    r = w_kv_up.shape[0]
    w = w_kv_up.reshape(r, MLA_HEADS, MLA_NOPE_DIM + MLA_V_DIM)
    k_nope = w[:, :, :MLA_NOPE_DIM].reshape(r, MLA_HEADS * MLA_NOPE_DIM)
    v = w[:, :, MLA_NOPE_DIM:].reshape(r, MLA_WIDTH)
    return jnp.concatenate([k_nope, v], axis=1).astype(BF16)


def _layer(x, cos, sin, pre_norm_w, w_in, q_norm_w, w_q_up, kv_norm_w, w_kv_up, w_out, post_norm_w):
    main, tail = _in_proj(x, pre_norm_w.reshape(1, -1), w_in.T, tm=1024)
    q_mla, k_mla, v_mla = _mla_prep(tail, cos, sin, q_norm_w.reshape(1, -1), kv_norm_w.reshape(1, -1),
                                    _prep_q_up_weight(w_q_up), _prep_kv_up_weight(w_kv_up), tm=1024)
    mix_a = _sb_attn(main, tq=256, tk=256, heads_per_step=8)
    mix_b = _mla_attn(q_mla, k_mla, v_mla, main, tk=512, heads_per_step=2)
    return _out_proj(mix_a, mix_b, w_out, x, post_norm_w.reshape(1, -1), tm=512)


def kernel(x, positions, pre_norm_w, w_in, q_norm_w, w_q_up, kv_norm_w, w_kv_up, w_out, post_norm_w):
    batch, depth = x.shape[0], pre_norm_w.shape[0]
    outs = []
    for b in range(batch):
        xb = x[b]
        cos, sin = _rope_tables(positions[b])
        for i in range(depth):
            xb = _layer(xb, cos, sin, pre_norm_w[i], w_in[i], q_norm_w[i], w_q_up[i],
                        kv_norm_w[i], w_kv_up[i], w_out[i], post_norm_w[i])
        outs.append(xb)
    return outs[0][None] if batch == 1 else jnp.stack(outs, axis=0)
```

```python
import functools
import math

import numpy as np
import jax
import jax.numpy as jnp
from jax import lax
from jax.experimental import pallas as pl
from jax.experimental.pallas import tpu as pltpu

F32 = jnp.float32
BF16 = jnp.bfloat16

LANES = 128
VMEM_LIMIT_BYTES = 56 * 1024 * 1024

CHUNK = 64
EPS = 1e-6
SB_HEADS = 8
SB_HEAD_DIM = 128
SB_WIDTH = SB_HEADS * SB_HEAD_DIM
MLA_HEADS = 8
MLA_NOPE_DIM = 128
MLA_ROPE_DIM = 64
MLA_QK_DIM = MLA_NOPE_DIM + MLA_ROPE_DIM
MLA_V_DIM = 128
MLA_WIDTH = MLA_HEADS * MLA_V_DIM
Q_LORA_RANK = 512
KV_LORA_RANK = 256
ROPE_THETA = 10000.0
HALF_ROPE = MLA_ROPE_DIM // 2
MLA_QK_PAD = 2 * LANES
LOG2_E = math.log2(math.e)
SB_SKIP_LOG2 = -150.0

MAIN_WIDTH = 4 * SB_WIDTH + MLA_WIDTH
TAIL_WIDTH = Q_LORA_RANK + KV_LORA_RANK + 2 * LANES
MLA_G_BLK = 4 * SB_WIDTH // LANES


def _rms_norm(x, w):
    return x * lax.rsqrt(jnp.mean(x * x, axis=-1, keepdims=True) + EPS) * w


def _silu(g):
    return g * (1.0 / (1.0 + jnp.exp(-g)))


def _dot_nt(a, b):
    return lax.dot_general(a, b, (((1,), (1,)), ((), ())), preferred_element_type=F32)


IN_TILE = 1024
N_SB = 4 * SB_WIDTH // IN_TILE
N_MAIN = MAIN_WIDTH // IN_TILE
N_IN_TILES = N_MAIN + TAIL_WIDTH // IN_TILE
C_Q_ROW = 4 * SB_WIDTH
MLA_GATE_ROW = C_Q_ROW + Q_LORA_RANK + KV_LORA_RANK + MLA_ROPE_DIM
assert SB_WIDTH % IN_TILE == 0 and MLA_WIDTH % IN_TILE == 0 and TAIL_WIDTH % IN_TILE == 0


def _in_proj_kernel(x_ref, nw_ref, wt_ref, main_ref, tail_ref, h_ref, *, q_scale):
    j = pl.program_id(1)

    @pl.when(j == 0)
    def _():
        h_ref[...] = _rms_norm(x_ref[...], nw_ref[...]).astype(BF16)

    def proj():
        return _dot_nt(h_ref[...], wt_ref[...].astype(BF16))

    @pl.when(j < SB_WIDTH // IN_TILE)
    def _():
        main_ref[...] = (proj() * q_scale).astype(BF16)

    @pl.when((j >= SB_WIDTH // IN_TILE) & (j < N_MAIN))
    def _():
        main_ref[...] = proj().astype(BF16)

    @pl.when(j >= N_MAIN)
    def _():
        tail_ref[...] = proj()


def _in_proj(x, norm_w, w_in_t, *, tm):
    s, d = x.shape
    kern = functools.partial(_in_proj_kernel, q_scale=LOG2_E / math.sqrt(SB_HEAD_DIM))

    def weight_row(i, j):
        unit = MLA_ROPE_DIM
        step = IN_TILE // unit
        row = jnp.where(j < N_SB, j * step,
                        jnp.where(j < N_MAIN, MLA_GATE_ROW // unit + (j - N_SB) * step,
                                  C_Q_ROW // unit + (j - N_MAIN) * step))
        return (row * unit, 0)

    return pl.pallas_call(
        kern,
        grid=(s // tm, N_IN_TILES),
        in_specs=[
            pl.BlockSpec((tm, d), lambda i, j: (i, 0)),
            pl.BlockSpec((1, d), lambda i, j: (0, 0)),
            pl.BlockSpec((pl.Element(IN_TILE), pl.Element(d)), weight_row),
        ],
        out_specs=[
            pl.BlockSpec((tm, IN_TILE), lambda i, j: (i, jnp.minimum(j, N_MAIN - 1))),
            pl.BlockSpec((tm, IN_TILE), lambda i, j: (i, jnp.maximum(j - N_MAIN, 0))),
        ],
        out_shape=[jax.ShapeDtypeStruct((s, MAIN_WIDTH), BF16),
                   jax.ShapeDtypeStruct((s, TAIL_WIDTH), F32)],
        scratch_shapes=[pltpu.VMEM((tm, d), BF16)],
        compiler_params=pltpu.CompilerParams(
            dimension_semantics=("arbitrary", "arbitrary"), vmem_limit_bytes=VMEM_LIMIT_BYTES),
        name="in_proj",
    )(x, norm_w, w_in_t)


def _rope_table_kernel(pos_ref, freq_ref, cos_ref, sin_ref):
    ang = freq_ref[...] * pos_ref[...].astype(F32)
    cos = jnp.cos(ang)
    sin = jnp.sin(ang)
    cos_ref[...] = jnp.concatenate([cos] * (LANES // HALF_ROPE), axis=0)
    sin_ref[...] = jnp.concatenate([sin] * (LANES // HALF_ROPE), axis=0)


def _rope_tables(positions):
    s = positions.shape[0]
    inv_freq = ROPE_THETA ** (-jnp.arange(0, MLA_ROPE_DIM, 2, dtype=F32) / MLA_ROPE_DIM)
    shape = jax.ShapeDtypeStruct((LANES, s), F32)
    return pl.pallas_call(_rope_table_kernel, out_shape=[shape, shape], name="rope_tab")(
        positions.reshape(1, s), inv_freq.reshape(HALF_ROPE, 1))


def _mla_prep_kernel(tail_ref, cos_ref, sin_ref, qnw_ref, kvnw_ref, wq_ref, wkv_ref,
                     q_ref, k_ref, v_ref, *, q_scale):
    cos = cos_ref[...].T
    sin = sin_ref[...].T
    lane = lax.broadcasted_iota(jnp.int32, cos.shape, 1)
    first_half = lane < MLA_ROPE_DIM
    c_q = tail_ref[:, 0:Q_LORA_RANK]
    hq = _rms_norm(c_q, qnw_ref[...]).astype(BF16)
    zq = jnp.dot(hq, wq_ref[...], preferred_element_type=F32)
    rope0 = MLA_HEADS * MLA_NOPE_DIM
    rot0 = rope0 + MLA_HEADS * MLA_ROPE_DIM
    for pair in range(MLA_HEADS // 2):
        lanes = slice(pair * LANES, (pair + 1) * LANES)
        roped = (zq[:, rope0:rot0][:, lanes] * cos + zq[:, rot0:][:, lanes] * sin)
        for h, mine in ((2 * pair, roped), (2 * pair + 1, pltpu.roll(roped, MLA_ROPE_DIM, axis=1))):
            lo = h * MLA_QK_PAD
            q_ref[:, lo:lo + LANES] = (zq[:, h * LANES:(h + 1) * LANES] * q_scale).astype(BF16)
            q_ref[:, lo + LANES:lo + 2 * LANES] = (
                jnp.where(first_half, mine, 0.0) * q_scale).astype(BF16)

    c_kv = tail_ref[:, Q_LORA_RANK:Q_LORA_RANK + KV_LORA_RANK]
    hkv = _rms_norm(c_kv, kvnw_ref[...]).astype(BF16)
    zkv = jnp.dot(hkv, wkv_ref[...], preferred_element_type=F32)
    kr0 = Q_LORA_RANK + KV_LORA_RANK
    kr = tail_ref[:, kr0:kr0 + LANES]
    rotated = jnp.where(lane < HALF_ROPE,
                        -pltpu.roll(kr, LANES - HALF_ROPE, axis=1),
                        pltpu.roll(kr, HALF_ROPE, axis=1))
    k_rot = jnp.where(first_half, kr * cos + rotated * sin, 0.0).astype(BF16)
    for h in range(MLA_HEADS):
        lo = h * MLA_QK_PAD
        k_ref[:, lo:lo + LANES] = zkv[:, h * LANES:(h + 1) * LANES].astype(BF16)
        k_ref[:, lo + LANES:lo + 2 * LANES] = k_rot
    v_ref[...] = zkv[:, MLA_HEADS * MLA_NOPE_DIM:].astype(BF16)


def _mla_prep(tail, cos, sin, q_norm_w, kv_norm_w, wq, wkv, *, tm):
    s = tail.shape[0]
    row = lambda i: (i, 0)
    fixed = lambda i: (0, 0)
    kern = functools.partial(_mla_prep_kernel, q_scale=LOG2_E / math.sqrt(MLA_QK_DIM))
    return pl.pallas_call(
        kern,
        grid=(s // tm,),
        in_specs=[
            pl.BlockSpec((tm, TAIL_WIDTH), row),
            pl.BlockSpec((LANES, tm), lambda i: (0, i)),
            pl.BlockSpec((LANES, tm), lambda i: (0, i)),
            pl.BlockSpec(q_norm_w.shape, fixed),
            pl.BlockSpec(kv_norm_w.shape, fixed),
            pl.BlockSpec(wq.shape, fixed),
            pl.BlockSpec(wkv.shape, fixed),
        ],
        out_specs=[
            pl.BlockSpec((tm, MLA_HEADS * MLA_QK_PAD), row),
            pl.BlockSpec((tm, MLA_HEADS * MLA_QK_PAD), row),
            pl.BlockSpec((tm, MLA_WIDTH), row),
        ],
        out_shape=[jax.ShapeDtypeStruct((s, MLA_HEADS * MLA_QK_PAD), BF16),
                   jax.ShapeDtypeStruct((s, MLA_HEADS * MLA_QK_PAD), BF16),
                   jax.ShapeDtypeStruct((s, MLA_WIDTH), BF16)],
        compiler_params=pltpu.CompilerParams(
            dimension_semantics=("arbitrary",), vmem_limit_bytes=VMEM_LIMIT_BYTES),
        name="mla_prep",
    )(tail, cos, sin, q_norm_w, kv_norm_w, wq, wkv)


def _sb_attn_kernel(q_ref, k_ref, v_ref, g_ref, o_ref, tri_ref, acc_ref, carry_ref, *, tq, tk):
    i = pl.program_id(1)
    n_heads = q_ref.shape[1] // SB_HEAD_DIM

    rows = lax.broadcasted_iota(jnp.int32, (tk, tk), 0)
    cols = lax.broadcasted_iota(jnp.int32, (tk, tk), 1)
    tri = jnp.where(rows >= cols, -1.0, 0.0).astype(BF16)
    tri_ref[0:tk, :] = tri
    tri_ref[tk:2 * tk, :] = tri
    acc_ref[...] = jnp.zeros_like(acc_ref)
    carry_ref[...] = jnp.zeros_like(carry_ref)

    def block(start, masked):
        heads = range(n_heads)
        lanes = [slice(h * SB_HEAD_DIM, (h + 1) * SB_HEAD_DIM) for h in heads]
        z = [_dot_nt(q_ref[:, lanes[h]], k_ref[pl.ds(start, tk), lanes[h]]) for h in heads]
        if masked:
            t_idx = i * tq + lax.broadcasted_iota(jnp.int32, (tq, tk), 0)
            s_idx = start + lax.broadcasted_iota(jnp.int32, (tq, tk), 1)
            before = s_idx < t_idx
            z = [jnp.where(before, z[h], -jnp.inf) for h in heads]
        sp = [jnp.maximum(z[h], 0.0) + jnp.log2(1.0 + jnp.exp2(-jnp.abs(z[h]))) for h in heads]
        tri = tri_ref[...]
        sums = []
        for h in heads:
            hi = sp[h].astype(BF16)
            lo = (sp[h] - hi.astype(F32)).astype(BF16)
            sums.append(jnp.dot(jnp.concatenate([hi, lo], axis=1), tri,
                                preferred_element_type=F32))
        carry = [carry_ref[:, lanes[h]] for h in heads]
        a = [jnp.exp2(z[h] + sums[h] + jnp.concatenate([carry[h]] * (tk // LANES), axis=1))
             for h in heads]
        for h in heads:
            acc_ref[:, lanes[h]] += jnp.dot(a[h].astype(BF16), v_ref[pl.ds(start, tk), lanes[h]],
                                            preferred_element_type=F32)
            carry_ref[:, lanes[h]] = carry[h] + jnp.broadcast_to(sums[h][:, 0:1], (tq, LANES))

    n_diag = tq // tk
    for d in reversed(range(n_diag)):
        block(pl.multiple_of(i * tq + d * tk, tk), True)

    def live():
        return (jnp.max(carry_ref[...]) > SB_SKIP_LOG2).astype(jnp.int32)

    def cond(state):
        j, alive = state
        return (j >= 0) & (alive > 0)

    def body(state):
        j, _ = state
        block(pl.multiple_of(j * tk, tk), False)
        return j - 1, live()

    lax.while_loop(cond, body, (i * n_diag - 1, jnp.int32(1)))
    o_ref[...] = (acc_ref[...] * _silu(g_ref[...].astype(F32))).astype(o_ref.dtype)


def _sb_attn(main, *, tq, tk, heads_per_step):
    s = main.shape[0]
    n_groups = SB_HEADS // heads_per_step
    width = heads_per_step * SB_HEAD_DIM
    kern = functools.partial(_sb_attn_kernel, tq=tq, tk=tk)
    resident = pl.Buffered(1) if n_groups == 1 else None
    return pl.pallas_call(
        kern,
        grid=(n_groups, s // tq),
        in_specs=[
            pl.BlockSpec((tq, width), lambda g, i: (i, g)),
            pl.BlockSpec((s, width), lambda g, i: (0, n_groups + g), pipeline_mode=resident),
            pl.BlockSpec((s, width), lambda g, i: (0, 2 * n_groups + g), pipeline_mode=resident),
            pl.BlockSpec((tq, width), lambda g, i: (i, 3 * n_groups + g)),
        ],
        out_specs=pl.BlockSpec((tq, width), lambda g, i: (i, g)),
        out_shape=jax.ShapeDtypeStruct((s, SB_WIDTH), BF16),
        scratch_shapes=[pltpu.VMEM((2 * tk, tk), BF16),
                        pltpu.VMEM((tq, width), F32),
                        pltpu.VMEM((tq, width), F32)],
        compiler_params=pltpu.CompilerParams(
            dimension_semantics=("arbitrary", "arbitrary"), vmem_limit_bytes=VMEM_LIMIT_BYTES),
        name="sb_attn",
    )(main, main, main, main)


def _mla_attn_kernel(q_ref, k_ref, v_ref, g_ref, o_ref, s_ref, m_ref, acc_ref, *, tk):
    i = pl.program_id(1)
    tq = 2 * tk
    n_col = tk // LANES
    n_heads = v_ref.shape[1] // MLA_V_DIM
    heads = range(n_heads)
    row_tile = 512
    top, bottom, both = slice(0, tk), slice(tk, tq), slice(0, tq)
    ones = jnp.ones((tk, LANES), BF16)
    m_ref[...] = jnp.full_like(m_ref, -jnp.inf)
    acc_ref[...] = jnp.zeros_like(acc_ref)

    def qk_lanes(head):
        return slice(head * MLA_QK_PAD, (head + 1) * MLA_QK_PAD)

    def v_lanes(head):
        return slice(head * MLA_V_DIM, (head + 1) * MLA_V_DIM)

    def scores(head, blk, slot, rows):
        start = pl.multiple_of(blk * tk, tk)
        s_ref[head, slot, rows] = _dot_nt(q_ref[rows, qk_lanes(head)],
                                          k_ref[pl.ds(start, tk), qk_lanes(head)])

    def tiles(blk, slot, rows, masked, next_blk=None):
        return [(head, blk, slot, slice(r0, r0 + row_tile), r0 - rows.start, masked, next_blk)
                for r0 in range(rows.start, rows.stop, row_tile) for head in heads]

    def softmax_stage(item):
        head, blk, slot, rows, row_offset, masked, _ = item

        def column(c):
            s = s_ref[head, slot, rows, c * LANES:(c + 1) * LANES]
            if masked:
                t_chunk = (row_offset + lax.broadcasted_iota(jnp.int32, (row_tile, LANES), 0)) // CHUNK
                s_chunk = (c * LANES + lax.broadcasted_iota(jnp.int32, (row_tile, LANES), 1)) // CHUNK
                s = jnp.where(s_chunk <= t_chunk, s, -jnp.inf)
            return s

        col_max = column(0)
        for c in range(1, n_col):
            col_max = jnp.maximum(col_max, column(c))
        m_prev = m_ref[head, rows]
        m_new = jnp.maximum(m_prev, jnp.max(col_max, axis=-1, keepdims=True))
        m_ref[head, rows] = m_new
        p = jnp.concatenate([jnp.exp2((column(c) - m_new).astype(BF16)) for c in range(n_col)],
                            axis=1)
        return head, blk, rows, jnp.exp2(m_prev - m_new), p

    def value_stage(head, blk, rows, alpha, p):
        start = pl.multiple_of(blk * tk, tk)
        v_ones = jnp.concatenate([v_ref[pl.ds(start, tk), v_lanes(head)], ones], axis=1)
        acc_ref[head, rows] = (jnp.concatenate([alpha, alpha], axis=1) * acc_ref[head, rows]
                               + jnp.dot(p, v_ones, preferred_element_type=F32))

    def fold(items):
        for item in items:
            head, _, slot, rows, _, _, next_blk = item
            if next_blk is not None:
                scores(head, next_blk, 1 - slot, rows)
            value_stage(*softmax_stage(item))

    for head in heads:
        scores(head, 0, 0, both)

    def pair(blk):
        return (tiles(blk, 0, both, False, next_blk=blk + 1)
                + tiles(blk + 1, 1, both, False, next_blk=blk + 2))

    def four_pairs(n, _):
        fold(pair(8 * n) + pair(8 * n + 2) + pair(8 * n + 4) + pair(8 * n + 6))
        return 0

    lax.fori_loop(0, lax.shift_right_logical(i, 2), four_pairs, 0)
    done = 8 * lax.shift_right_logical(i, 2)

    @pl.when((i & 2) != 0)
    def _():
        fold(pair(done) + pair(done + 2))

    @pl.when((i & 1) != 0)
    def _():
        fold(pair(2 * i - 2))

    for head in heads:
        scores(head, 2 * i + 1, 1, bottom)
    fold(tiles(2 * i, 0, top, True) + tiles(2 * i, 0, bottom, False)
         + tiles(2 * i + 1, 1, bottom, True))

    for head in heads:
        out = acc_ref[head, :, 0:MLA_V_DIM] * (1.0 / acc_ref[head, :, MLA_V_DIM:])
        gate = g_ref[:, v_lanes(head)].astype(F32)
        o_ref[:, v_lanes(head)] = (out * _silu(gate)).astype(o_ref.dtype)


def _mla_attn(q, k, v, main, *, tk, heads_per_step):
    s = q.shape[0]
    tq = 2 * tk
    n_groups = MLA_HEADS // heads_per_step
    qk_width = heads_per_step * MLA_QK_PAD
    v_width = heads_per_step * MLA_V_DIM
    gate_block = MLA_G_BLK // heads_per_step
    kern = functools.partial(_mla_attn_kernel, tk=tk)
    return pl.pallas_call(
        kern,
        grid=(n_groups, s // tq),
        in_specs=[
            pl.BlockSpec((tq, qk_width), lambda g, i: (i, g)),
            pl.BlockSpec((s, qk_width), lambda g, i: (0, g)),
            pl.BlockSpec((s, v_width), lambda g, i: (0, g)),
            pl.BlockSpec((tq, v_width), lambda g, i: (i, gate_block + g)),
        ],
        out_specs=pl.BlockSpec((tq, v_width), lambda g, i: (i, g)),
        out_shape=jax.ShapeDtypeStruct((s, MLA_WIDTH), BF16),
        scratch_shapes=[pltpu.VMEM((heads_per_step, 2, tq, tk), F32),
                        pltpu.VMEM((heads_per_step, tq, LANES), F32),
                        pltpu.VMEM((heads_per_step, tq, MLA_V_DIM + LANES), F32)],
        compiler_params=pltpu.CompilerParams(
            dimension_semantics=("arbitrary", "arbitrary"), vmem_limit_bytes=VMEM_LIMIT_BYTES),
        name="mla_attn",
    )(q, k, v, main)


def _out_proj_kernel(a_ref, b_ref, w_ref, x_ref, nw_ref, o_ref, wb_ref):
    @pl.when(pl.program_id(0) == 0)
    def _():
        wb_ref[...] = w_ref[...].astype(BF16)

    y = (jnp.dot(a_ref[...], wb_ref[0:SB_WIDTH, :], preferred_element_type=F32)
         + jnp.dot(b_ref[...], wb_ref[SB_WIDTH:, :], preferred_element_type=F32))
    o_ref[...] = x_ref[...] + _rms_norm(y, nw_ref[...])


def _out_proj(mix_a, mix_b, w_out, x, norm_w, *, tm):
    s, d = x.shape
    row = lambda i: (i, 0)
    fixed = lambda i: (0, 0)
    return pl.pallas_call(
        _out_proj_kernel,
        grid=(s // tm,),
        in_specs=[
            pl.BlockSpec((tm, SB_WIDTH), row),
            pl.BlockSpec((tm, MLA_WIDTH), row),
            pl.BlockSpec(w_out.shape, fixed, pipeline_mode=pl.Buffered(1)),
            pl.BlockSpec((tm, d), row),
            pl.BlockSpec((1, d), fixed),
        ],
        out_specs=pl.BlockSpec((tm, d), row),
        out_shape=jax.ShapeDtypeStruct((s, d), F32),
        scratch_shapes=[pltpu.VMEM(w_out.shape, BF16)],
        compiler_params=pltpu.CompilerParams(
            dimension_semantics=("arbitrary",), vmem_limit_bytes=VMEM_LIMIT_BYTES),
        name="out_proj",
    )(mix_a, mix_b, w_out, x, norm_w)


def _rotate_half_cols(w):
    w1, w2 = jnp.split(w, 2, axis=-1)
    return jnp.concatenate([-w2, w1], axis=-1)


def _prep_q_up_weight(w_q_up):
    r = w_q_up.shape[0]
    w = w_q_up.reshape(r, MLA_HEADS, MLA_QK_DIM)
    nope, rope = w[:, :, :MLA_NOPE_DIM], w[:, :, MLA_NOPE_DIM:]
    groups = [nope, rope, _rotate_half_cols(rope)]
    return jnp.concatenate([g.reshape(r, -1) for g in groups], axis=1).astype(BF16)


def _prep_kv_up_weight(w_kv_up):
    r = w_kv_up.shape[0]
    w = w_kv_up.reshape(r, MLA_HEADS, MLA_NOPE_DIM + MLA_V_DIM)
    k_nope = w[:, :, :MLA_NOPE_DIM].reshape(r, MLA_HEADS * MLA_NOPE_DIM)
    v = w[:, :, MLA_NOPE_DIM:].reshape(r, MLA_WIDTH)
    return jnp.concatenate([k_nope, v], axis=1).astype(BF16)


def _layer(x, cos, sin, pre_norm_w, w_in, q_norm_w, w_q_up, kv_norm_w, w_kv_up, w_out, post_norm_w):
    main, tail = _in_proj(x, pre_norm_w.reshape(1, -1), w_in.T, tm=1024)
    q_mla, k_mla, v_mla = _mla_prep(tail, cos, sin, q_norm_w.reshape(1, -1), kv_norm_w.reshape(1, -1),
                                    _prep_q_up_weight(w_q_up), _prep_kv_up_weight(w_kv_up), tm=1024)
    mix_a = _sb_attn(main, tq=256, tk=256, heads_per_step=8)
    mix_b = _mla_attn(q_mla, k_mla, v_mla, main, tk=512, heads_per_step=2)
    return _out_proj(mix_a, mix_b, w_out, x, post_norm_w.reshape(1, -1), tm=512)


def kernel(x, positions, pre_norm_w, w_in, q_norm_w, w_q_up, kv_norm_w, w_kv_up, w_out, post_norm_w):
    batch, depth = x.shape[0], pre_norm_w.shape[0]
    outs = []
    for b in range(batch):
        xb = x[b]
        cos, sin = _rope_tables(positions[b])
        for i in range(depth):
            xb = _layer(xb, cos, sin, pre_norm_w[i], w_in[i], q_norm_w[i], w_q_up[i],
                        kv_norm_w[i], w_kv_up[i], w_out[i], post_norm_w[i])
        outs.append(xb)
    return outs[0][None] if batch == 1 else jnp.stack(outs, axis=0)
```
